```python
import math
import jax, jax.numpy as jnp
from jax import lax
import numpy as np

D_MODEL = 1024
BATCH = 8
SEQ = 2048
DEPTH = 2

MIX_WIDTH = D_MODEL
ATTN_WIDTH = MIX_WIDTH // 2
POOL_WIDTH = MIX_WIDTH - ATTN_WIDTH
N_DIFF_HEADS = 4
DIFF_HEAD_DIM = ATTN_WIDTH // N_DIFF_HEADS // 2
DIFF_V_DIM = 2 * DIFF_HEAD_DIM
POOL_WINDOWS = (2, 4, 8, 16)
N_POOL_GROUPS = len(POOL_WINDOWS)
POOL_GROUP_DIM = POOL_WIDTH // N_POOL_GROUPS
IN_WIDTH = 3 * ATTN_WIDTH + POOL_WIDTH
FF_MULTIPLE = 256
D_FF = ((8 * D_MODEL + 3 * FF_MULTIPLE - 1) // (3 * FF_MULTIPLE)) * FF_MULTIPLE
Q_BLOCK = 128
NORM_EPS = 1e-5

kernel_name = "hymba_diffattn_multiscale_pool_block"


def rms_norm(x, g):
    xf = x.astype(jnp.float32)
    y = xf * lax.rsqrt(jnp.mean(xf * xf, axis=-1, keepdims=True) + NORM_EPS)
    return (y * g.astype(jnp.float32)).astype(x.dtype)


def lambda_init(layer):
    return 0.8 - 0.6 * math.exp(-0.3 * layer)


def diff_attention(q, k, v, lam):
    seq = q.shape[1]
    scale = DIFF_HEAD_DIM ** -0.5
    q1 = jnp.transpose(q[:, :, :, 0, :], (0, 2, 1, 3)) * scale
    q2 = jnp.transpose(q[:, :, :, 1, :], (0, 2, 1, 3)) * scale
    k1 = jnp.transpose(k[:, :, :, 0, :], (0, 2, 1, 3))
    k2 = jnp.transpose(k[:, :, :, 1, :], (0, 2, 1, 3))
    vh = jnp.transpose(v, (0, 2, 1, 3)).astype(jnp.float32)
    outs = []
    for blk in range(seq // Q_BLOCK):
        start = blk * Q_BLOCK
        end = start + Q_BLOCK
        mask = jnp.arange(end)[None, :] <= (start + jnp.arange(Q_BLOCK))[:, None]
        s1 = jnp.einsum('bhqd,bhkd->bhqk', q1[:, :, start:end], k1[:, :, :end]).astype(jnp.float32)
        s2 = jnp.einsum('bhqd,bhkd->bhqk', q2[:, :, start:end], k2[:, :, :end]).astype(jnp.float32)
        s1 = jnp.where(mask, s1, -jnp.inf)
        s2 = jnp.where(mask, s2, -jnp.inf)
        attn = jax.nn.softmax(s1, axis=-1) - lam * jax.nn.softmax(s2, axis=-1)
        outs.append(jnp.einsum('bhqk,bhkv->bhqv', attn, vh[:, :, :end]))
    o = jnp.concatenate(outs, axis=2)
    return jnp.transpose(o, (0, 2, 1, 3)).astype(v.dtype)


def causal_multiscale_pool(u, pool_w, pool_scale):
    seq = u.shape[1]
    pos_count = (jnp.arange(seq) + 1)[None, :, None]
    ys = []
    for g, w in enumerate(POOL_WINDOWS):
        ug = u[..., g * POOL_GROUP_DIM:(g + 1) * POOL_GROUP_DIM].astype(jnp.float32)
        cs = jnp.cumsum(ug, axis=1)
        prev = jnp.pad(cs, ((0, 0), (w, 0), (0, 0)))[:, :seq]
        count = jnp.minimum(pos_count, w).astype(jnp.float32)
        resid = (cs - prev) / count - ug
        ys.append(jnp.einsum('bsc,cd->bsd', resid, pool_w[g].astype(jnp.float32)))
    y = jnp.concatenate(ys, axis=-1) * pool_scale.astype(jnp.float32)
    return y.astype(u.dtype)


def setup_inputs(seed: int = 0) -> dict:
    key = jax.random.key(seed)
    ks = jax.random.split(key, 14)
    f32 = jnp.float32
    def nrm(k, shape, fan_in):
        return jax.random.normal(k, shape, f32) * (fan_in ** -0.5)
    def gain(k, shape):
        return 1.0 + 0.02 * jax.random.normal(k, shape, f32)
    return {
        "x": jax.random.normal(ks[0], (BATCH, SEQ, D_MODEL), f32),
        "norm1_g": gain(ks[1], (DEPTH, D_MODEL)),
        "w_in": nrm(ks[2], (DEPTH, D_MODEL, IN_WIDTH), D_MODEL),
        "lam_qk": 0.1 * jax.random.normal(ks[3], (DEPTH, 4, DIFF_HEAD_DIM), f32),
        "subln_g": gain(ks[4], (DEPTH, DIFF_V_DIM)),
        "pool_w": nrm(ks[5], (DEPTH, N_POOL_GROUPS, POOL_GROUP_DIM, POOL_GROUP_DIM), POOL_GROUP_DIM),
        "pool_scale": gain(ks[6], (DEPTH, POOL_WIDTH)),
        "w_out": nrm(ks[7], (DEPTH, MIX_WIDTH, D_MODEL), MIX_WIDTH),
        "norm2_g": gain(ks[8], (DEPTH, D_MODEL)),
        "w_gate": nrm(ks[9], (DEPTH, D_MODEL, D_FF), D_MODEL),
        "w_up": nrm(ks[10], (DEPTH, D_MODEL, D_FF), D_MODEL),
        "w_down": nrm(ks[11], (DEPTH, D_FF, D_MODEL), D_FF),
        "final_g": gain(ks[12], (D_MODEL,)),
    }


def reference(x, norm1_g, w_in, lam_qk, subln_g, pool_w, pool_scale, w_out,
              norm2_g, w_gate, w_up, w_down, final_g):
    bsz, seq, _ = x.shape
    for layer in range(DEPTH):
        lam0 = lambda_init(layer)
        h = rms_norm(x, norm1_g[layer])
        proj = jnp.einsum('bsd,de->bse', h, w_in[layer])
        q = proj[..., :ATTN_WIDTH].reshape(bsz, seq, N_DIFF_HEADS, 2, DIFF_HEAD_DIM)
        k = proj[..., ATTN_WIDTH:2 * ATTN_WIDTH].reshape(bsz, seq, N_DIFF_HEADS, 2, DIFF_HEAD_DIM)
        v = proj[..., 2 * ATTN_WIDTH:3 * ATTN_WIDTH].reshape(bsz, seq, N_DIFF_HEADS, DIFF_V_DIM)
        u = proj[..., 3 * ATTN_WIDTH:]
        lq = lam_qk[layer].astype(jnp.float32)
        lam = (jnp.exp(jnp.sum(lq[0] * lq[1])) - jnp.exp(jnp.sum(lq[2] * lq[3])) + lam0)
        a = diff_attention(q, k, v, lam)
        a = (rms_norm(a, subln_g[layer]) * (1.0 - lam0)).reshape(bsz, seq, ATTN_WIDTH)
        p = causal_multiscale_pool(u, pool_w[layer], pool_scale[layer])
        mixed = jnp.concatenate([a, p], axis=-1)
        x = x + jnp.einsum('bse,ed->bsd', mixed, w_out[layer])
        h2 = rms_norm(x, norm2_g[layer])
        gate = jnp.einsum('bsd,df->bsf', h2, w_gate[layer])
        up = jnp.einsum('bsd,df->bsf', h2, w_up[layer])
        x = x + jnp.einsum('bsf,fd->bsd', jax.nn.silu(gate) * up, w_down[layer])
    return rms_norm(x, final_g)
```

```python
import functools
import math

import jax
import jax.numpy as jnp
from jax import lax
from jax.experimental import pallas as pl
from jax.experimental.pallas import tpu as pltpu

F32 = jnp.float32
BF16 = jnp.bfloat16

N_DIFF_HEADS = 4
DIFF_HEAD_DIM = 64
DIFF_V_DIM = 2 * DIFF_HEAD_DIM
POOL_WINDOWS = (2, 4, 8, 16)
POOL_GROUP_DIM = 128
MAX_WINDOW = max(POOL_WINDOWS)
NORM_EPS = 1e-5

VMEM_LIMIT_BYTES = 56 * 1024 * 1024

TOKEN_TILE = 512
Q_TILE = 256
K_TILE = 256
FF_CHUNKS = (1536, 1280)


def _lambda_init(layer):
    return 0.8 - 0.6 * math.exp(-0.3 * layer)


def _rms_scale(x):
    return lax.rsqrt(jnp.mean(x * x, axis=-1, keepdims=True) + NORM_EPS)


def _resident(shape):
    return pl.BlockSpec(shape, lambda *_: (0,) * len(shape), pipeline_mode=pl.Buffered(1))


def _norm_inproj_kernel(x_ref, g_ref, w_ref, o_ref):
    x = x_ref[...]
    h = (x * _rms_scale(x) * g_ref[...]).astype(BF16)
    o_ref[...] = jnp.dot(h, w_ref[...], preferred_element_type=F32).astype(BF16)


def _norm_inproj(x2d, g, w_bf16):
    n, d = x2d.shape
    e = w_bf16.shape[1]
    return pl.pallas_call(
        _norm_inproj_kernel,
        grid=(n // TOKEN_TILE,),
        in_specs=[
            pl.BlockSpec((TOKEN_TILE, d), lambda i: (i, 0)),
            _resident((1, d)),
            _resident((d, e)),
        ],
        out_specs=pl.BlockSpec((TOKEN_TILE, e), lambda i: (i, 0)),
        out_shape=jax.ShapeDtypeStruct((n, e), BF16),
        compiler_params=pltpu.CompilerParams(
            dimension_semantics=("arbitrary",), vmem_limit_bytes=VMEM_LIMIT_BYTES),
        name="norm_inproj",
    )(x2d, g.reshape(1, d), w_bf16)


def _attn_kernel(lamqk_ref, q_ref, k_ref, v_ref, sg_ref, o_ref,
                 vt_ref, m_ref, l_ref, acc_ref, *, lam0):
    qi = pl.program_id(2)

    @pl.when(qi == 0)
    def _():
        vt_ref[...] = v_ref[...].astype(F32).T.astype(BF16)

    lq = lamqk_ref[...]
    lam = (jnp.exp(jnp.sum(lq[0:1] * lq[1:2], axis=-1, keepdims=True))
           - jnp.exp(jnp.sum(lq[2:3] * lq[3:4], axis=-1, keepdims=True)) + lam0)

    q = q_ref[...] * jnp.asarray(DIFF_HEAD_DIM ** -0.5, BF16)
    lane = lax.broadcasted_iota(jnp.int32, q.shape, 1)
    zero = jnp.zeros_like(q)
    q_comp = (jnp.where(lane < DIFF_HEAD_DIM, q, zero), jnp.where(lane >= DIFF_HEAD_DIM, q, zero))

    m_ref[...] = jnp.full(m_ref.shape, -jnp.inf, F32)
    l_ref[...] = jnp.zeros(l_ref.shape, F32)
    acc_ref[...] = jnp.zeros(acc_ref.shape, F32)

    def block(kj, masked):
        k0 = pl.multiple_of(kj * K_TILE, K_TILE)
        kb = k_ref[pl.ds(k0, K_TILE), :]
        vtb = vt_ref[:, pl.ds(k0, K_TILE)]
        if masked:
            key_pos = k0 + lax.broadcasted_iota(jnp.int32, (K_TILE, Q_TILE), 0)
            qry_pos = qi * Q_TILE + lax.broadcasted_iota(jnp.int32, (K_TILE, Q_TILE), 1)
            keep = key_pos <= qry_pos
        for c in range(2):
            s = lax.dot_general(kb, q_comp[c], (((1,), (1,)), ((), ())),
                                preferred_element_type=F32)
            if masked:
                s = jnp.where(keep, s, -jnp.inf)
            m_old = m_ref[c]
            m_new = jnp.maximum(m_old, jnp.max(s, axis=0, keepdims=True))
            alpha = jnp.exp(m_old - m_new)
            p = jnp.exp(s - m_new)
            l_ref[c] = alpha * l_ref[c] + jnp.sum(p, axis=0, keepdims=True)
            acc_ref[c] = alpha * acc_ref[c] + jnp.dot(vtb, p.astype(BF16),
                                                      preferred_element_type=F32)
            m_ref[c] = m_new

    n_full = qi * (Q_TILE // K_TILE)

    def body(kj, carry):
        block(kj, masked=False)
        return carry

    lax.fori_loop(0, n_full, body, 0)
    for d in range(Q_TILE // K_TILE):
        block(n_full + d, masked=True)

    o = acc_ref[0] / l_ref[0] - lam * (acc_ref[1] / l_ref[1])
    o = o * lax.rsqrt(jnp.mean(o * o, axis=0, keepdims=True) + NORM_EPS)
    o_ref[...] = (o.T * sg_ref[...] * (1.0 - lam0)).astype(BF16)


def _diff_attention(proj, lam_qk, subln_g, *, batch, seq, lam0):
    n = batch * seq
    nq = seq // Q_TILE
    col_blocks = N_DIFF_HEADS
    return pl.pallas_call(
        functools.partial(_attn_kernel, lam0=lam0),
        grid=(batch, N_DIFF_HEADS, nq),
        in_specs=[
            _resident((4, DIFF_HEAD_DIM)),
            pl.BlockSpec((Q_TILE, DIFF_V_DIM), lambda b, h, i: (b * nq + i, h)),
            pl.BlockSpec((seq, DIFF_V_DIM), lambda b, h, i: (b, col_blocks + h)),
            pl.BlockSpec((seq, DIFF_V_DIM), lambda b, h, i: (b, 2 * col_blocks + h)),
            _resident((1, DIFF_V_DIM)),
        ],
        out_specs=pl.BlockSpec((Q_TILE, DIFF_V_DIM), lambda b, h, i: (b * nq + i, h)),
        out_shape=jax.ShapeDtypeStruct((n, N_DIFF_HEADS * DIFF_V_DIM), BF16),
        scratch_shapes=[
            pltpu.VMEM((DIFF_V_DIM, seq), BF16),
            pltpu.VMEM((2, 1, Q_TILE), F32),
            pltpu.VMEM((2, 1, Q_TILE), F32),
            pltpu.VMEM((2, DIFF_V_DIM, Q_TILE), F32),
        ],
        compiler_params=pltpu.CompilerParams(
            dimension_semantics=("arbitrary", "arbitrary", "arbitrary"),
            vmem_limit_bytes=VMEM_LIMIT_BYTES),
        name="diff_attention",
    )(lam_qk, proj, proj, proj, subln_g.reshape(1, DIFF_V_DIM))


def _mix_ffn_kernel(x_ref, a_ref, u_ref, halo_ref, pw_ref, ps_ref, wo_ref, g2_ref,
                    wg_ref, wu_ref, wd_ref, gf_ref, o_ref, ext_ref, *, tiles_per_seq, final_norm):
    tm = x_ref.shape[0]
    t0 = (pl.program_id(0) % tiles_per_seq) * tm

    u = u_ref[...].astype(F32)
    halo = halo_ref[...].astype(F32)
    ext_ref[0:MAX_WINDOW, :] = jnp.where(t0 == 0, jnp.zeros_like(halo), halo)
    ext_ref[MAX_WINDOW:, :] = u
    pos = t0 + lax.broadcasted_iota(jnp.int32, (tm, 1), 0)
    ys = []
    for g, w in enumerate(POOL_WINDOWS):
        cols = slice(g * POOL_GROUP_DIM, (g + 1) * POOL_GROUP_DIM)
        wsum = ext_ref[MAX_WINDOW:MAX_WINDOW + tm, cols]
        for k in range(1, w):
            wsum = wsum + ext_ref[MAX_WINDOW - k:MAX_WINDOW - k + tm, cols]
        inv_count = 1.0 / jnp.minimum(pos + 1, w).astype(F32)
        resid = wsum * inv_count - u[:, cols]
        ys.append(jnp.dot(resid.astype(BF16), pw_ref[g], preferred_element_type=F32))
    y = jnp.concatenate(ys, axis=-1) * ps_ref[...]

    mixed = jnp.concatenate([a_ref[...], y.astype(BF16)], axis=-1)
    x1 = x_ref[...] + jnp.dot(mixed, wo_ref[...], preferred_element_type=F32)

    h2 = (x1 * _rms_scale(x1) * g2_ref[...]).astype(BF16)
    ffn = None
    f0 = 0
    for fc in FF_CHUNKS:
        gate = jnp.dot(h2, wg_ref[:, f0:f0 + fc], preferred_element_type=F32)
        up = jnp.dot(h2, wu_ref[:, f0:f0 + fc], preferred_element_type=F32)
        act = (gate * (1.0 / (1.0 + jnp.exp(-gate))) * up).astype(BF16)
        part = jnp.dot(act, wd_ref[f0:f0 + fc, :], preferred_element_type=F32)
        ffn = part if ffn is None else ffn + part
        f0 += fc
    x2 = x1 + ffn
    if final_norm:
        x2 = x2 * _rms_scale(x2) * gf_ref[...]
    o_ref[...] = x2


def _mix_ffn(x2d, attn, proj, pool_w, pool_scale, w_out, g2, w_gate, w_up, w_down, final_g,
             *, seq, final_norm):
    n, d = x2d.shape
    d_ff = w_gate.shape[1]
    assert sum(FF_CHUNKS) == d_ff
    pool_width = len(POOL_WINDOWS) * POOL_GROUP_DIM
    tm = TOKEN_TILE
    u_col = proj.shape[1] // pool_width - 1
    halo_per_tile = tm // MAX_WINDOW
    return pl.pallas_call(
        functools.partial(_mix_ffn_kernel, tiles_per_seq=seq // tm, final_norm=final_norm),
        grid=(n // tm,),
        in_specs=[
            pl.BlockSpec((tm, d), lambda i: (i, 0)),
            pl.BlockSpec((tm, attn.shape[1]), lambda i: (i, 0)),
            pl.BlockSpec((tm, pool_width), lambda i: (i, u_col)),
            pl.BlockSpec((MAX_WINDOW, pool_width),
                         lambda i: (jnp.maximum(i * halo_per_tile - 1, 0), u_col)),
            _resident(pool_w.shape),
            _resident((1, pool_width)),
            _resident(w_out.shape),
            _resident((1, d)),
            _resident(w_gate.shape),
            _resident(w_up.shape),
            _resident(w_down.shape),
            _resident((1, d)),
        ],
        out_specs=pl.BlockSpec((tm, d), lambda i: (i, 0)),
        out_shape=jax.ShapeDtypeStruct((n, d), F32),
        scratch_shapes=[pltpu.VMEM((MAX_WINDOW + tm, pool_width), F32)],
        compiler_params=pltpu.CompilerParams(
            dimension_semantics=("arbitrary",), vmem_limit_bytes=VMEM_LIMIT_BYTES),
        name="mix_ffn",
    )(x2d, attn, proj, proj, pool_w, pool_scale.reshape(1, pool_width), w_out,
      g2.reshape(1, d), w_gate, w_up, w_down, final_g.reshape(1, d))


def kernel(x, norm1_g, w_in, lam_qk, subln_g, pool_w, pool_scale, w_out, norm2_g, w_gate, w_up,
           w_down, final_g):
    batch, seq, d = x.shape
    depth = w_in.shape[0]
    assert seq % TOKEN_TILE == 0 and seq % Q_TILE == 0 and Q_TILE % K_TILE == 0
    x2d = x.reshape(batch * seq, d)
    for layer in range(depth):
        proj = _norm_inproj(x2d, norm1_g[layer], w_in[layer].astype(BF16))
        attn = _diff_attention(proj, lam_qk[layer], subln_g[layer], batch=batch, seq=seq,
                               lam0=_lambda_init(layer))
        x2d = _mix_ffn(x2d, attn, proj, pool_w[layer].astype(BF16), pool_scale[layer],
                       w_out[layer].astype(BF16), norm2_g[layer], w_gate[layer].astype(BF16),
                       w_up[layer].astype(BF16), w_down[layer].astype(BF16), final_g,
                       seq=seq, final_norm=(layer == depth - 1))
    return x2d.reshape(batch, seq, d)
```

```python
import functools
import math

import jax
import jax.numpy as jnp
from jax import lax
from jax.experimental import pallas as pl
from jax.experimental.pallas import tpu as pltpu

F32 = jnp.float32
BF16 = jnp.bfloat16

N_DIFF_HEADS = 4
DIFF_HEAD_DIM = 64
DIFF_V_DIM = 2 * DIFF_HEAD_DIM
POOL_WINDOWS = (2, 4, 8, 16)
POOL_GROUP_DIM = 128
MAX_WINDOW = max(POOL_WINDOWS)
NORM_EPS = 1e-5

VMEM_LIMIT_BYTES = 56 * 1024 * 1024

TOKEN_TILE = 512
Q_TILE = 512
DIAG_BLOCK = 256
FF_CHUNKS = (1536, 1280)


def _lambda_init(layer):
    return 0.8 - 0.6 * math.exp(-0.3 * layer)


def _rms_scale(x):
    return lax.rsqrt(jnp.mean(x * x, axis=-1, keepdims=True) + NORM_EPS)


def _resident(shape):
    return pl.BlockSpec(shape, lambda *_: (0,) * len(shape), pipeline_mode=pl.Buffered(1))


def _norm_inproj_kernel(x_ref, g_ref, w_ref, o_ref):
    x = x_ref[...]
    h = (x * _rms_scale(x) * g_ref[...]).astype(BF16)
    o_ref[...] = jnp.dot(h, w_ref[...], preferred_element_type=F32).astype(BF16)


def _norm_inproj(x2d, g, w_bf16):
    n, d = x2d.shape
    e = w_bf16.shape[1]
    return pl.pallas_call(
        _norm_inproj_kernel,
        grid=(n // TOKEN_TILE,),
        in_specs=[
            pl.BlockSpec((TOKEN_TILE, d), lambda i: (i, 0)),
            _resident((1, d)),
            _resident((d, e)),
        ],
        out_specs=pl.BlockSpec((TOKEN_TILE, e), lambda i: (i, 0)),
        out_shape=jax.ShapeDtypeStruct((n, e), BF16),
        compiler_params=pltpu.CompilerParams(
            dimension_semantics=("arbitrary",), vmem_limit_bytes=VMEM_LIMIT_BYTES),
        name="norm_inproj",
    )(x2d, g.reshape(1, d), w_bf16)


def _attn_kernel(lamqk_ref, q_ref, k_ref, v_ref, sg_ref, o_ref,
                 vt_ref, s_ref, acc_ref, *, lam0):
    qi = pl.program_id(2)
    tq = Q_TILE

    @pl.when(qi == 0)
    def _():
        vt_ref[...] = v_ref[...].astype(F32).T.astype(BF16)

    lq = lamqk_ref[...]
    lam = (jnp.exp(jnp.sum(lq[0:1] * lq[1:2], axis=-1, keepdims=True))
           - jnp.exp(jnp.sum(lq[2:3] * lq[3:4], axis=-1, keepdims=True)) + lam0)

    q = q_ref[...] * jnp.asarray(DIFF_HEAD_DIM ** -0.5, BF16)
    lane = lax.broadcasted_iota(jnp.int32, q.shape, 1)
    zero = jnp.zeros_like(q)
    q_comp = (jnp.where(lane < DIFF_HEAD_DIM, q, zero), jnp.where(lane >= DIFF_HEAD_DIM, q, zero))

    def scores_t(kb, qc):
        return lax.dot_general(kb, qc, (((1,), (1,)), ((), ())), preferred_element_type=F32)

    def fold8(x, op):
        return op(x.reshape(-1, 8, x.shape[-1]), axis=0)

    def merge_cols(old, new, lo, op):
        if lo == 0:
            return op(old, new)
        return jnp.concatenate([old[:, :lo], op(old[:, lo:], new)], axis=1)

    q0 = pl.multiple_of(qi * tq, tq)
    diag = [(lo, tq - lo) for lo in range(0, tq, DIAG_BLOCK)]

    def pass1(kj, mparts):
        k0 = pl.multiple_of(kj * tq, tq)
        kb = k_ref[pl.ds(k0, tq), :]
        out = []
        for c in range(2):
            s = scores_t(kb, q_comp[c])
            s_ref[c, pl.ds(k0, tq), :] = s
            out.append(jnp.maximum(mparts[c], fold8(s, jnp.max)))
        return tuple(out)

    neg = jnp.full((8, tq), -jnp.inf, F32)
    mparts = list(lax.fori_loop(0, qi, pass1, (neg, neg)))
    for lo, w in diag:
        kb = k_ref[pl.ds(q0 + lo, DIAG_BLOCK), :]
        keep = (lax.broadcasted_iota(jnp.int32, (DIAG_BLOCK, w), 0)
                <= lax.broadcasted_iota(jnp.int32, (DIAG_BLOCK, w), 1))
        for c in range(2):
            s = jnp.where(keep, scores_t(kb, q_comp[c][lo:]), -jnp.inf)
            s_ref[c, pl.ds(q0 + lo, DIAG_BLOCK), lo:] = s
            mparts[c] = merge_cols(mparts[c], fold8(s, jnp.max), lo, jnp.maximum)
    m = [jnp.max(mp, axis=0, keepdims=True) for mp in mparts]

    acc_ref[...] = jnp.zeros(acc_ref.shape, F32)

    def pass2(kj, lparts):
        k0 = pl.multiple_of(kj * tq, tq)
        vtb = vt_ref[:, pl.ds(k0, tq)]
        out = []
        for c in range(2):
            p = jnp.exp(s_ref[c, pl.ds(k0, tq), :] - m[c])
            out.append(lparts[c] + fold8(p, jnp.sum))
            acc_ref[c] += jnp.dot(vtb, p.astype(BF16), preferred_element_type=F32)
        return tuple(out)

    zeros8 = jnp.zeros((8, tq), F32)
    lparts = list(lax.fori_loop(0, qi, pass2, (zeros8, zeros8)))
    for lo, w in diag:
        vtb = vt_ref[:, pl.ds(q0 + lo, DIAG_BLOCK)]
        for c in range(2):
            p = jnp.exp(s_ref[c, pl.ds(q0 + lo, DIAG_BLOCK), lo:] - m[c][:, lo:])
            lparts[c] = merge_cols(lparts[c], fold8(p, jnp.sum), lo, jnp.add)
            acc_ref[c, :, lo:] += jnp.dot(vtb, p.astype(BF16), preferred_element_type=F32)
    l = [jnp.sum(lp, axis=0, keepdims=True) for lp in lparts]

    o = acc_ref[0] * (1.0 / l[0]) - acc_ref[1] * (lam / l[1])
    o = o * lax.rsqrt(jnp.mean(o * o, axis=0, keepdims=True) + NORM_EPS)
    o_ref[...] = (o.T * sg_ref[...] * (1.0 - lam0)).astype(BF16)


def _diff_attention(proj, lam_qk, subln_g, *, batch, seq, lam0):
    n = batch * seq
    nq = seq // Q_TILE
    col_blocks = N_DIFF_HEADS
    return pl.pallas_call(
        functools.partial(_attn_kernel, lam0=lam0),
        grid=(batch, N_DIFF_HEADS, nq),
        in_specs=[
            _resident((4, DIFF_HEAD_DIM)),
            pl.BlockSpec((Q_TILE, DIFF_V_DIM), lambda b, h, i: (b * nq + i, h)),
            pl.BlockSpec((seq, DIFF_V_DIM), lambda b, h, i: (b, col_blocks + h)),
            pl.BlockSpec((seq, DIFF_V_DIM), lambda b, h, i: (b, 2 * col_blocks + h)),
            _resident((1, DIFF_V_DIM)),
        ],
        out_specs=pl.BlockSpec((Q_TILE, DIFF_V_DIM), lambda b, h, i: (b * nq + i, h)),
        out_shape=jax.ShapeDtypeStruct((n, N_DIFF_HEADS * DIFF_V_DIM), BF16),
        scratch_shapes=[
            pltpu.VMEM((DIFF_V_DIM, seq), BF16),
            pltpu.VMEM((2, seq, Q_TILE), F32),
            pltpu.VMEM((2, DIFF_V_DIM, Q_TILE), F32),
        ],
        compiler_params=pltpu.CompilerParams(
            dimension_semantics=("arbitrary", "arbitrary", "arbitrary"),
            vmem_limit_bytes=VMEM_LIMIT_BYTES),
        name="diff_attention",
    )(lam_qk, proj, proj, proj, subln_g.reshape(1, DIFF_V_DIM))


def _mix_ffn_kernel(x_ref, a_ref, u_ref, halo_ref, pw_ref, ps_ref, wo_ref, g2_ref,
                    wg_ref, wu_ref, wd_ref, gf_ref, o_ref, ext_ref, *, tiles_per_seq, final_norm):
    tm = x_ref.shape[0]
    t0 = (pl.program_id(0) % tiles_per_seq) * tm

    u = u_ref[...].astype(F32)
    halo = halo_ref[...].astype(F32)
    ext_ref[0:MAX_WINDOW, :] = jnp.where(t0 == 0, jnp.zeros_like(halo), halo)
    ext_ref[MAX_WINDOW:, :] = u
    pos = t0 + lax.broadcasted_iota(jnp.int32, (tm, 1), 0)
    ys = []
    for g, w in enumerate(POOL_WINDOWS):
        cols = slice(g * POOL_GROUP_DIM, (g + 1) * POOL_GROUP_DIM)
        wsum = ext_ref[MAX_WINDOW:MAX_WINDOW + tm, cols]
        for k in range(1, w):
            wsum = wsum + ext_ref[MAX_WINDOW - k:MAX_WINDOW - k + tm, cols]
        inv_count = 1.0 / jnp.minimum(pos + 1, w).astype(F32)
        resid = wsum * inv_count - u[:, cols]
        ys.append(jnp.dot(resid.astype(BF16), pw_ref[g], preferred_element_type=F32))
    y = jnp.concatenate(ys, axis=-1) * ps_ref[...]

    mixed = jnp.concatenate([a_ref[...], y.astype(BF16)], axis=-1)
    x1 = x_ref[...] + jnp.dot(mixed, wo_ref[...], preferred_element_type=F32)

    h2 = (x1 * _rms_scale(x1) * g2_ref[...]).astype(BF16)
    ffn = None
    f0 = 0
    for fc in FF_CHUNKS:
        gate = jnp.dot(h2, wg_ref[:, f0:f0 + fc], preferred_element_type=F32)
        up = jnp.dot(h2, wu_ref[:, f0:f0 + fc], preferred_element_type=F32)
        act = (gate * (1.0 / (1.0 + jnp.exp(-gate))) * up).astype(BF16)
        part = jnp.dot(act, wd_ref[f0:f0 + fc, :], preferred_element_type=F32)
        ffn = part if ffn is None else ffn + part
        f0 += fc
    x2 = x1 + ffn
    if final_norm:
        x2 = x2 * _rms_scale(x2) * gf_ref[...]
    o_ref[...] = x2


def _mix_ffn(x2d, attn, proj, pool_w, pool_scale, w_out, g2, w_gate, w_up, w_down, final_g,
             *, seq, final_norm):
    n, d = x2d.shape
    d_ff = w_gate.shape[1]
    assert sum(FF_CHUNKS) == d_ff
    pool_width = len(POOL_WINDOWS) * POOL_GROUP_DIM
    tm = TOKEN_TILE
    u_col = proj.shape[1] // pool_width - 1
    halo_per_tile = tm // MAX_WINDOW
    return pl.pallas_call(
        functools.partial(_mix_ffn_kernel, tiles_per_seq=seq // tm, final_norm=final_norm),
        grid=(n // tm,),
        in_specs=[
            pl.BlockSpec((tm, d), lambda i: (i, 0)),
            pl.BlockSpec((tm, attn.shape[1]), lambda i: (i, 0)),
            pl.BlockSpec((tm, pool_width), lambda i: (i, u_col)),
            pl.BlockSpec((MAX_WINDOW, pool_width),
                         lambda i: (jnp.maximum(i * halo_per_tile - 1, 0), u_col)),
            _resident(pool_w.shape),
            _resident((1, pool_width)),
            _resident(w_out.shape),
            _resident((1, d)),
            _resident(w_gate.shape),
            _resident(w_up.shape),
            _resident(w_down.shape),
            _resident((1, d)),
        ],
        out_specs=pl.BlockSpec((tm, d), lambda i: (i, 0)),
        out_shape=jax.ShapeDtypeStruct((n, d), F32),
        scratch_shapes=[pltpu.VMEM((MAX_WINDOW + tm, pool_width), F32)],
        compiler_params=pltpu.CompilerParams(
            dimension_semantics=("arbitrary",), vmem_limit_bytes=VMEM_LIMIT_BYTES),
        name="mix_ffn",
    )(x2d, attn, proj, proj, pool_w, pool_scale.reshape(1, pool_width), w_out,
      g2.reshape(1, d), w_gate, w_up, w_down, final_g.reshape(1, d))


def kernel(x, norm1_g, w_in, lam_qk, subln_g, pool_w, pool_scale, w_out, norm2_g, w_gate, w_up,
           w_down, final_g):
    batch, seq, d = x.shape
    depth = w_in.shape[0]
    assert seq % TOKEN_TILE == 0 and seq % Q_TILE == 0 and Q_TILE % DIAG_BLOCK == 0
    x2d = x.reshape(batch * seq, d)
    for layer in range(depth):
        proj = _norm_inproj(x2d, norm1_g[layer], w_in[layer].astype(BF16))
        attn = _diff_attention(proj, lam_qk[layer], subln_g[layer], batch=batch, seq=seq,
                               lam0=_lambda_init(layer))
        x2d = _mix_ffn(x2d, attn, proj, pool_w[layer].astype(BF16), pool_scale[layer],
                       w_out[layer].astype(BF16), norm2_g[layer], w_gate[layer].astype(BF16),
                       w_up[layer].astype(BF16), w_down[layer].astype(BF16), final_g,
                       seq=seq, final_norm=(layer == depth - 1))
    return x2d.reshape(batch, seq, d)
```

```python
import functools
import math

import jax
import jax.numpy as jnp
from jax import lax
from jax.experimental import pallas as pl
from jax.experimental.pallas import tpu as pltpu

F32 = jnp.float32
BF16 = jnp.bfloat16

N_DIFF_HEADS = 4
DIFF_HEAD_DIM = 64
DIFF_V_DIM = 2 * DIFF_HEAD_DIM
POOL_WINDOWS = (2, 4, 8, 16)
POOL_GROUP_DIM = 128
MAX_WINDOW = max(POOL_WINDOWS)
NORM_EPS = 1e-5

VMEM_LIMIT_BYTES = 56 * 1024 * 1024

TOKEN_TILE = 512
Q_TILE = 512
FF_CHUNKS = (1536, 1280)


def _lambda_init(layer):
    return 0.8 - 0.6 * math.exp(-0.3 * layer)


def _rms_scale(x):
    return lax.rsqrt(jnp.mean(x * x, axis=-1, keepdims=True) + NORM_EPS)


def _resident(shape):
    return pl.BlockSpec(shape, lambda *_: (0,) * len(shape), pipeline_mode=pl.Buffered(1))


def _norm_inproj_kernel(x_ref, g_ref, w_ref, o_ref):
    x = x_ref[...]
    h = (x * _rms_scale(x) * g_ref[...]).astype(BF16)
    o_ref[...] = jnp.dot(h, w_ref[...], preferred_element_type=F32).astype(BF16)


def _norm_inproj(x2d, g, w_bf16):
    n, d = x2d.shape
    e = w_bf16.shape[1]
    return pl.pallas_call(
        _norm_inproj_kernel,
        grid=(n // TOKEN_TILE,),
        in_specs=[
            pl.BlockSpec((TOKEN_TILE, d), lambda i: (i, 0)),
            _resident((1, d)),
            _resident((d, e)),
        ],
        out_specs=pl.BlockSpec((TOKEN_TILE, e), lambda i: (i, 0)),
        out_shape=jax.ShapeDtypeStruct((n, e), BF16),
        compiler_params=pltpu.CompilerParams(
            dimension_semantics=("arbitrary",), vmem_limit_bytes=VMEM_LIMIT_BYTES),
        name="norm_inproj",
    )(x2d, g.reshape(1, d), w_bf16)


def _attn_kernel(lamqk_ref, q_ref, k_ref, v_ref, sg_ref, o_ref, vt_ref, *, lam0):
    seq = q_ref.shape[0]
    tq = Q_TILE
    half = tq // 2

    vt_ref[...] = v_ref[...].astype(F32).T.astype(BF16)

    lq = lamqk_ref[...]
    lam = (jnp.exp(jnp.sum(lq[0:1] * lq[1:2], axis=-1, keepdims=True))
           - jnp.exp(jnp.sum(lq[2:3] * lq[3:4], axis=-1, keepdims=True)) + lam0)

    def scores_t(kb, qc):
        return lax.dot_general(kb, qc, (((1,), (1,)), ((), ())), preferred_element_type=F32)

    def fold8(x, op):
        return op(x.reshape(-1, 8, x.shape[-1]), axis=0)

    def upper_cols(wide, narrow, op):
        return jnp.concatenate([wide[:, :half], op(wide[:, half:], narrow)], axis=1)

    keep_wide = (lax.broadcasted_iota(jnp.int32, (half, tq), 0)
                 <= lax.broadcasted_iota(jnp.int32, (half, tq), 1))
    keep_narrow = keep_wide[:, :half]

    for t in range(seq // tq):
        q0 = t * tq
        q = q_ref[q0:q0 + tq, :] * jnp.asarray(DIFF_HEAD_DIM ** -0.5, BF16)
        lane = lax.broadcasted_iota(jnp.int32, q.shape, 1)
        zero = jnp.zeros_like(q)
        q_comp = (jnp.where(lane < DIFF_HEAD_DIM, q, zero), jnp.where(lane >= DIFF_HEAD_DIM, q, zero))

        o_comp = []
        for c in range(2):
            s_diag = jnp.where(keep_wide, scores_t(k_ref[q0:q0 + half, :], q_comp[c]), -jnp.inf)
            s_last = jnp.where(keep_narrow,
                               scores_t(k_ref[q0 + half:q0 + tq, :], q_comp[c][half:]), -jnp.inf)
            mpart = fold8(s_diag, jnp.max)
            if q0:
                s_full = scores_t(k_ref[0:q0, :], q_comp[c])
                mpart = jnp.maximum(mpart, fold8(s_full, jnp.max))
            mpart = upper_cols(mpart, fold8(s_last, jnp.max), jnp.maximum)
            m = jnp.max(mpart, axis=0, keepdims=True)

            p_diag = jnp.exp(s_diag - m)
            p_last = jnp.exp(s_last - m[:, half:])
            lpart = fold8(p_diag, jnp.sum)
            p_wide = p_diag.astype(BF16)
            if q0:
                p_full = jnp.exp(s_full - m)
                lpart = lpart + fold8(p_full, jnp.sum)
                p_wide = jnp.concatenate([p_full.astype(BF16), p_wide], axis=0)
            lpart = upper_cols(lpart, fold8(p_last, jnp.sum), jnp.add)
            l = jnp.sum(lpart, axis=0, keepdims=True)

            acc = jnp.dot(vt_ref[:, 0:q0 + half], p_wide, preferred_element_type=F32)
            acc = upper_cols(acc, jnp.dot(vt_ref[:, q0 + half:q0 + tq], p_last.astype(BF16),
                                          preferred_element_type=F32), jnp.add)
            o_comp.append((acc, l))

        (acc1, l1), (acc2, l2) = o_comp
        o = acc1 * (1.0 / l1) - acc2 * (lam / l2)
        o = o * lax.rsqrt(jnp.mean(o * o, axis=0, keepdims=True) + NORM_EPS)
        o_ref[q0:q0 + tq, :] = (o.T * sg_ref[...] * (1.0 - lam0)).astype(BF16)


def _diff_attention(proj, lam_qk, subln_g, *, batch, seq, lam0):
    n = batch * seq
    col_blocks = N_DIFF_HEADS
    return pl.pallas_call(
        functools.partial(_attn_kernel, lam0=lam0),
        grid=(batch, N_DIFF_HEADS),
        in_specs=[
            _resident((4, DIFF_HEAD_DIM)),
            pl.BlockSpec((seq, DIFF_V_DIM), lambda b, h: (b, h)),
            pl.BlockSpec((seq, DIFF_V_DIM), lambda b, h: (b, col_blocks + h)),
            pl.BlockSpec((seq, DIFF_V_DIM), lambda b, h: (b, 2 * col_blocks + h)),
            _resident((1, DIFF_V_DIM)),
        ],
        out_specs=pl.BlockSpec((seq, DIFF_V_DIM), lambda b, h: (b, h)),
        out_shape=jax.ShapeDtypeStruct((n, N_DIFF_HEADS * DIFF_V_DIM), BF16),
        scratch_shapes=[pltpu.VMEM((DIFF_V_DIM, seq), BF16)],
        compiler_params=pltpu.CompilerParams(
            dimension_semantics=("arbitrary", "arbitrary"), vmem_limit_bytes=VMEM_LIMIT_BYTES),
        name="diff_attention",
    )(lam_qk, proj, proj, proj, subln_g.reshape(1, DIFF_V_DIM))


def _mix_ffn_kernel(x_ref, a_ref, u_ref, halo_ref, pw_ref, ps_ref, wo_ref, g2_ref,
                    wg_ref, wu_ref, wd_ref, gf_ref, o_ref, ext_ref, *, tiles_per_seq, final_norm):
    tm = x_ref.shape[0]
    t0 = (pl.program_id(0) % tiles_per_seq) * tm

    u = u_ref[...].astype(F32)
    halo = halo_ref[...].astype(F32)
    ext_ref[0:MAX_WINDOW, :] = jnp.where(t0 == 0, jnp.zeros_like(halo), halo)
    ext_ref[MAX_WINDOW:, :] = u
    pos = t0 + lax.broadcasted_iota(jnp.int32, (tm, 1), 0)
    ys = []
    for g, w in enumerate(POOL_WINDOWS):
        cols = slice(g * POOL_GROUP_DIM, (g + 1) * POOL_GROUP_DIM)
        wsum = ext_ref[MAX_WINDOW:MAX_WINDOW + tm, cols]
        for k in range(1, w):
            wsum = wsum + ext_ref[MAX_WINDOW - k:MAX_WINDOW - k + tm, cols]
        inv_count = 1.0 / jnp.minimum(pos + 1, w).astype(F32)
        resid = wsum * inv_count - u[:, cols]
        ys.append(jnp.dot(resid.astype(BF16), pw_ref[g], preferred_element_type=F32))
    y = jnp.concatenate(ys, axis=-1) * ps_ref[...]

    mixed = jnp.concatenate([a_ref[...], y.astype(BF16)], axis=-1)
    x1 = x_ref[...] + jnp.dot(mixed, wo_ref[...], preferred_element_type=F32)

    h2 = (x1 * _rms_scale(x1) * g2_ref[...]).astype(BF16)
    ffn = None
    f0 = 0
    for fc in FF_CHUNKS:
        gate = jnp.dot(h2, wg_ref[:, f0:f0 + fc], preferred_element_type=F32)
        up = jnp.dot(h2, wu_ref[:, f0:f0 + fc], preferred_element_type=F32)
        act = (gate * (1.0 / (1.0 + jnp.exp(-gate))) * up).astype(BF16)
        part = jnp.dot(act, wd_ref[f0:f0 + fc, :], preferred_element_type=F32)
        ffn = part if ffn is None else ffn + part
        f0 += fc
    x2 = x1 + ffn
    if final_norm:
        x2 = x2 * _rms_scale(x2) * gf_ref[...]
    o_ref[...] = x2


def _mix_ffn(x2d, attn, proj, pool_w, pool_scale, w_out, g2, w_gate, w_up, w_down, final_g,
             *, seq, final_norm):
    n, d = x2d.shape
    d_ff = w_gate.shape[1]
    assert sum(FF_CHUNKS) == d_ff
    pool_width = len(POOL_WINDOWS) * POOL_GROUP_DIM
    tm = TOKEN_TILE
    u_col = proj.shape[1] // pool_width - 1
    halo_per_tile = tm // MAX_WINDOW
    return pl.pallas_call(
        functools.partial(_mix_ffn_kernel, tiles_per_seq=seq // tm, final_norm=final_norm),
        grid=(n // tm,),
        in_specs=[
            pl.BlockSpec((tm, d), lambda i: (i, 0)),
            pl.BlockSpec((tm, attn.shape[1]), lambda i: (i, 0)),
            pl.BlockSpec((tm, pool_width), lambda i: (i, u_col)),
            pl.BlockSpec((MAX_WINDOW, pool_width),
                         lambda i: (jnp.maximum(i * halo_per_tile - 1, 0), u_col)),
            _resident(pool_w.shape),
            _resident((1, pool_width)),
            _resident(w_out.shape),
            _resident((1, d)),
            _resident(w_gate.shape),
            _resident(w_up.shape),
            _resident(w_down.shape),
            _resident((1, d)),
        ],
        out_specs=pl.BlockSpec((tm, d), lambda i: (i, 0)),
        out_shape=jax.ShapeDtypeStruct((n, d), F32),
        scratch_shapes=[pltpu.VMEM((MAX_WINDOW + tm, pool_width), F32)],
        compiler_params=pltpu.CompilerParams(
            dimension_semantics=("arbitrary",), vmem_limit_bytes=VMEM_LIMIT_BYTES),
        name="mix_ffn",
    )(x2d, attn, proj, proj, pool_w, pool_scale.reshape(1, pool_width), w_out,
      g2.reshape(1, d), w_gate, w_up, w_down, final_g.reshape(1, d))


def kernel(x, norm1_g, w_in, lam_qk, subln_g, pool_w, pool_scale, w_out, norm2_g, w_gate, w_up,
           w_down, final_g):
    batch, seq, d = x.shape
    depth = w_in.shape[0]
    assert seq % TOKEN_TILE == 0 and seq % Q_TILE == 0
    x2d = x.reshape(batch * seq, d)
    for layer in range(depth):
        proj = _norm_inproj(x2d, norm1_g[layer], w_in[layer].astype(BF16))
        attn = _diff_attention(proj, lam_qk[layer], subln_g[layer], batch=batch, seq=seq,
                               lam0=_lambda_init(layer))
        x2d = _mix_ffn(x2d, attn, proj, pool_w[layer].astype(BF16), pool_scale[layer],
                       w_out[layer].astype(BF16), norm2_g[layer], w_gate[layer].astype(BF16),
                       w_up[layer].astype(BF16), w_down[layer].astype(BF16), final_g,
                       seq=seq, final_norm=(layer == depth - 1))
    return x2d.reshape(batch, seq, d)
```

```python
import functools
import math

import jax
import jax.numpy as jnp
from jax import lax
from jax.experimental import pallas as pl
from jax.experimental.pallas import tpu as pltpu

F32 = jnp.float32
BF16 = jnp.bfloat16

N_DIFF_HEADS = 4
DIFF_HEAD_DIM = 64
DIFF_V_DIM = 2 * DIFF_HEAD_DIM
ATTN_WIDTH = N_DIFF_HEADS * DIFF_V_DIM
Q_SCALE = DIFF_HEAD_DIM ** -0.5 * math.log2(math.e)
POOL_WINDOWS = (2, 4, 8, 16)
POOL_GROUP_DIM = 128
MAX_WINDOW = max(POOL_WINDOWS)
NORM_EPS = 1e-5

VMEM_LIMIT_BYTES = 56 * 1024 * 1024

TOKEN_TILE = 512
Q_TILE = 512
FF_CHUNKS = (1536, 1280)


def _lambda_init(layer):
    return 0.8 - 0.6 * math.exp(-0.3 * layer)


def _rms_scale(x):
    return lax.rsqrt(jnp.mean(x * x, axis=-1, keepdims=True) + NORM_EPS)


def _resident(shape):
    return pl.BlockSpec(shape, lambda *_: (0,) * len(shape), pipeline_mode=pl.Buffered(1))


def _norm_inproj_kernel(x_ref, g_ref, w_ref, o_ref):
    x = x_ref[...]
    h = (x * _rms_scale(x) * g_ref[...]).astype(BF16)
    proj = jnp.dot(h, w_ref[...], preferred_element_type=F32)
    o_ref[:, :ATTN_WIDTH] = (proj[:, :ATTN_WIDTH] * Q_SCALE).astype(BF16)
    o_ref[:, ATTN_WIDTH:] = proj[:, ATTN_WIDTH:].astype(BF16)


def _norm_inproj(x2d, g, w_bf16):
    n, d = x2d.shape
    e = w_bf16.shape[1]
    return pl.pallas_call(
        _norm_inproj_kernel,
        grid=(n // TOKEN_TILE,),
        in_specs=[
            pl.BlockSpec((TOKEN_TILE, d), lambda i: (i, 0)),
            _resident((1, d)),
            _resident((d, e)),
        ],
        out_specs=pl.BlockSpec((TOKEN_TILE, e), lambda i: (i, 0)),
        out_shape=jax.ShapeDtypeStruct((n, e), BF16),
        compiler_params=pltpu.CompilerParams(
            dimension_semantics=("arbitrary",), vmem_limit_bytes=VMEM_LIMIT_BYTES),
        name="norm_inproj",
    )(x2d, g.reshape(1, d), w_bf16)


def _attn_kernel(lamqk_ref, q_ref, k_ref, v_ref, sg_ref, o_ref, vt_ref, *, lam0):
    seq = q_ref.shape[0]
    tq = Q_TILE
    half = tq // 2

    vt_ref[...] = v_ref[...].astype(F32).T.astype(BF16)

    lq = lamqk_ref[...]
    lam = (jnp.exp(jnp.sum(lq[0:1] * lq[1:2], axis=-1, keepdims=True))
           - jnp.exp(jnp.sum(lq[2:3] * lq[3:4], axis=-1, keepdims=True)) + lam0)

    def scores_t(kb, qc):
        return lax.dot_general(kb, qc, (((1,), (1,)), ((), ())), preferred_element_type=F32)

    def fold8(x, op):
        return op(x.reshape(-1, 8, x.shape[-1]), axis=0)

    def upper_cols(wide, narrow, op):
        return jnp.concatenate([wide[:, :half], op(wide[:, half:], narrow)], axis=1)

    keep_wide = (lax.broadcasted_iota(jnp.int32, (half, tq), 0)
                 <= lax.broadcasted_iota(jnp.int32, (half, tq), 1))
    keep_narrow = keep_wide[:, :half]

    def score_stage(t, c):
        q0 = t * tq
        q = q_ref[q0:q0 + tq, :]
        lane = lax.broadcasted_iota(jnp.int32, q.shape, 1)
        mine = (lane < DIFF_HEAD_DIM) if c == 0 else (lane >= DIFF_HEAD_DIM)
        qc = jnp.where(mine, q, jnp.zeros_like(q))
        s_diag = jnp.where(keep_wide, scores_t(k_ref[q0:q0 + half, :], qc), -jnp.inf)
        s_last = jnp.where(keep_narrow, scores_t(k_ref[q0 + half:q0 + tq, :], qc[half:]), -jnp.inf)
        mpart = fold8(s_diag, jnp.max)
        s_full = None
        if q0:
            s_full = scores_t(k_ref[0:q0, :], qc)
            mpart = jnp.maximum(mpart, fold8(s_full, jnp.max))
        mpart = upper_cols(mpart, fold8(s_last, jnp.max), jnp.maximum)
        return s_full, s_diag, s_last, jnp.max(mpart, axis=0, keepdims=True)

    def value_stage(t, scores):
        q0 = t * tq
        s_full, s_diag, s_last, m = scores
        p_diag = jnp.exp2(s_diag - m)
        p_last = jnp.exp2(s_last - m[:, half:])
        lpart = fold8(p_diag, jnp.sum)
        p_wide = p_diag.astype(BF16)
        if s_full is not None:
            p_full = jnp.exp2(s_full - m)
            lpart = lpart + fold8(p_full, jnp.sum)
            p_wide = jnp.concatenate([p_full.astype(BF16), p_wide], axis=0)
        lpart = upper_cols(lpart, fold8(p_last, jnp.sum), jnp.add)
        acc = jnp.dot(vt_ref[:, 0:q0 + half], p_wide, preferred_element_type=F32)
        acc = upper_cols(acc, jnp.dot(vt_ref[:, q0 + half:q0 + tq], p_last.astype(BF16),
                                      preferred_element_type=F32), jnp.add)
        return acc, jnp.sum(lpart, axis=0, keepdims=True)

    def finish_tile(t, comp1, comp2):
        (acc1, l1), (acc2, l2) = comp1, comp2
        o = acc1 * (1.0 / l1) - acc2 * (lam / l2)
        o = o * lax.rsqrt(jnp.mean(o * o, axis=0, keepdims=True) + NORM_EPS)
        o_ref[t * tq:(t + 1) * tq, :] = (o.T * sg_ref[...] * (1.0 - lam0)).astype(BF16)

    units = [(t, c) for t in range(seq // tq) for c in range(2)]
    scores = score_stage(*units[0])
    done = {}
    for i, (t, c) in enumerate(units):
        nxt = score_stage(*units[i + 1]) if i + 1 < len(units) else None
        done[c] = value_stage(t, scores)
        scores = nxt
        if c == 1:
            finish_tile(t, done[0], done[1])


def _diff_attention(proj, lam_qk, subln_g, *, batch, seq, lam0):
    n = batch * seq
    col_blocks = N_DIFF_HEADS
    return pl.pallas_call(
        functools.partial(_attn_kernel, lam0=lam0),
        grid=(batch, N_DIFF_HEADS),
        in_specs=[
            _resident((4, DIFF_HEAD_DIM)),
            pl.BlockSpec((seq, DIFF_V_DIM), lambda b, h: (b, h)),
            pl.BlockSpec((seq, DIFF_V_DIM), lambda b, h: (b, col_blocks + h)),
            pl.BlockSpec((seq, DIFF_V_DIM), lambda b, h: (b, 2 * col_blocks + h)),
            _resident((1, DIFF_V_DIM)),
        ],
        out_specs=pl.BlockSpec((seq, DIFF_V_DIM), lambda b, h: (b, h)),
        out_shape=jax.ShapeDtypeStruct((n, N_DIFF_HEADS * DIFF_V_DIM), BF16),
        scratch_shapes=[pltpu.VMEM((DIFF_V_DIM, seq), BF16)],
        compiler_params=pltpu.CompilerParams(
            dimension_semantics=("arbitrary", "arbitrary"), vmem_limit_bytes=VMEM_LIMIT_BYTES),
        name="diff_attention",
    )(lam_qk, proj, proj, proj, subln_g.reshape(1, DIFF_V_DIM))


def _mix_ffn_kernel(x_ref, a_ref, u_ref, halo_ref, pw_ref, ps_ref, wo_ref, g2_ref,
                    wg_ref, wu_ref, wd_ref, gf_ref, o_ref, ext_ref, *, tiles_per_seq, final_norm):
    tm = x_ref.shape[0]
    t0 = (pl.program_id(0) % tiles_per_seq) * tm

    u = u_ref[...].astype(F32)
    halo = halo_ref[...].astype(F32)
    ext_ref[0:MAX_WINDOW, :] = jnp.where(t0 == 0, jnp.zeros_like(halo), halo)
    ext_ref[MAX_WINDOW:, :] = u
    pos = t0 + lax.broadcasted_iota(jnp.int32, (tm, 1), 0)
    ys = []
    for g, w in enumerate(POOL_WINDOWS):
        cols = slice(g * POOL_GROUP_DIM, (g + 1) * POOL_GROUP_DIM)
        wsum = ext_ref[MAX_WINDOW:MAX_WINDOW + tm, cols]
        for k in range(1, w):
            wsum = wsum + ext_ref[MAX_WINDOW - k:MAX_WINDOW - k + tm, cols]
        inv_count = 1.0 / jnp.minimum(pos + 1, w).astype(F32)
        resid = wsum * inv_count - u[:, cols]
        ys.append(jnp.dot(resid.astype(BF16), pw_ref[g], preferred_element_type=F32))
    y = jnp.concatenate(ys, axis=-1) * ps_ref[...]

    mixed = jnp.concatenate([a_ref[...], y.astype(BF16)], axis=-1)
    x1 = x_ref[...] + jnp.dot(mixed, wo_ref[...], preferred_element_type=F32)

    h2 = (x1 * _rms_scale(x1) * g2_ref[...]).astype(BF16)
    ffn = None
    f0 = 0
    for fc in FF_CHUNKS:
        gate = jnp.dot(h2, wg_ref[:, f0:f0 + fc], preferred_element_type=F32)
        up = jnp.dot(h2, wu_ref[:, f0:f0 + fc], preferred_element_type=F32)
        act = (gate * (1.0 / (1.0 + jnp.exp(-gate))) * up).astype(BF16)
        part = jnp.dot(act, wd_ref[f0:f0 + fc, :], preferred_element_type=F32)
        ffn = part if ffn is None else ffn + part
        f0 += fc
    x2 = x1 + ffn
    if final_norm:
        x2 = x2 * _rms_scale(x2) * gf_ref[...]
    o_ref[...] = x2


def _mix_ffn(x2d, attn, proj, pool_w, pool_scale, w_out, g2, w_gate, w_up, w_down, final_g,
             *, seq, final_norm):
    n, d = x2d.shape
    d_ff = w_gate.shape[1]
    assert sum(FF_CHUNKS) == d_ff
    pool_width = len(POOL_WINDOWS) * POOL_GROUP_DIM
    tm = TOKEN_TILE
    u_col = proj.shape[1] // pool_width - 1
    halo_per_tile = tm // MAX_WINDOW
    return pl.pallas_call(
        functools.partial(_mix_ffn_kernel, tiles_per_seq=seq // tm, final_norm=final_norm),
        grid=(n // tm,),
        in_specs=[
            pl.BlockSpec((tm, d), lambda i: (i, 0)),
            pl.BlockSpec((tm, attn.shape[1]), lambda i: (i, 0)),
            pl.BlockSpec((tm, pool_width), lambda i: (i, u_col)),
            pl.BlockSpec((MAX_WINDOW, pool_width),
                         lambda i: (jnp.maximum(i * halo_per_tile - 1, 0), u_col)),
            _resident(pool_w.shape),
            _resident((1, pool_width)),
            _resident(w_out.shape),
            _resident((1, d)),
            _resident(w_gate.shape),
            _resident(w_up.shape),
            _resident(w_down.shape),
            _resident((1, d)),
        ],
        out_specs=pl.BlockSpec((tm, d), lambda i: (i, 0)),
        out_shape=jax.ShapeDtypeStruct((n, d), F32),
        scratch_shapes=[pltpu.VMEM((MAX_WINDOW + tm, pool_width), F32)],
        compiler_params=pltpu.CompilerParams(
            dimension_semantics=("arbitrary",), vmem_limit_bytes=VMEM_LIMIT_BYTES),
        name="mix_ffn",
    )(x2d, attn, proj, proj, pool_w, pool_scale.reshape(1, pool_width), w_out,
      g2.reshape(1, d), w_gate, w_up, w_down, final_g.reshape(1, d))


def kernel(x, norm1_g, w_in, lam_qk, subln_g, pool_w, pool_scale, w_out, norm2_g, w_gate, w_up,
           w_down, final_g):
    batch, seq, d = x.shape
    depth = w_in.shape[0]
    assert seq % TOKEN_TILE == 0 and seq % Q_TILE == 0
    x2d = x.reshape(batch * seq, d)
    for layer in range(depth):
        proj = _norm_inproj(x2d, norm1_g[layer], w_in[layer].astype(BF16))
        attn = _diff_attention(proj, lam_qk[layer], subln_g[layer], batch=batch, seq=seq,
                               lam0=_lambda_init(layer))
        x2d = _mix_ffn(x2d, attn, proj, pool_w[layer].astype(BF16), pool_scale[layer],
                       w_out[layer].astype(BF16), norm2_g[layer], w_gate[layer].astype(BF16),
                       w_up[layer].astype(BF16), w_down[layer].astype(BF16), final_g,
                       seq=seq, final_norm=(layer == depth - 1))
    return x2d.reshape(batch, seq, d)
```

```python
import functools
import math

import jax
import jax.numpy as jnp
from jax import lax
from jax.experimental import pallas as pl
from jax.experimental.pallas import tpu as pltpu

F32 = jnp.float32
BF16 = jnp.bfloat16

N_DIFF_HEADS = 4
DIFF_HEAD_DIM = 64
DIFF_V_DIM = 2 * DIFF_HEAD_DIM
ATTN_WIDTH = N_DIFF_HEADS * DIFF_V_DIM
Q_SCALE = DIFF_HEAD_DIM ** -0.5 * math.log2(math.e)
POOL_WINDOWS = (2, 4, 8, 16)
POOL_GROUP_DIM = 128
MAX_WINDOW = max(POOL_WINDOWS)
NORM_EPS = 1e-5

VMEM_LIMIT_BYTES = 56 * 1024 * 1024

TOKEN_TILE = 512
Q_TILE = 512
FF_CHUNKS = (1536, 1280)


def _lambda_init(layer):
    return 0.8 - 0.6 * math.exp(-0.3 * layer)


def _rms_scale(x):
    return lax.rsqrt(jnp.mean(x * x, axis=-1, keepdims=True) + NORM_EPS)


def _resident(shape):
    return pl.BlockSpec(shape, lambda *_: (0,) * len(shape), pipeline_mode=pl.Buffered(1))


def _norm_inproj_kernel(x_ref, g_ref, w_ref, o_ref):
    x = x_ref[...]
    proj = jnp.dot((x * g_ref[...]).astype(BF16), w_ref[...], preferred_element_type=F32)
    scale = _rms_scale(x)
    o_ref[:, :ATTN_WIDTH] = (proj[:, :ATTN_WIDTH] * (scale * Q_SCALE)).astype(BF16)
    o_ref[:, ATTN_WIDTH:] = (proj[:, ATTN_WIDTH:] * scale).astype(BF16)


def _norm_inproj(x2d, g, w_bf16):
    n, d = x2d.shape
    e = w_bf16.shape[1]
    return pl.pallas_call(
        _norm_inproj_kernel,
        grid=(n // TOKEN_TILE,),
        in_specs=[
            pl.BlockSpec((TOKEN_TILE, d), lambda i: (i, 0)),
            _resident((1, d)),
            _resident((d, e)),
        ],
        out_specs=pl.BlockSpec((TOKEN_TILE, e), lambda i: (i, 0)),
        out_shape=jax.ShapeDtypeStruct((n, e), BF16),
        compiler_params=pltpu.CompilerParams(
            dimension_semantics=("arbitrary",), vmem_limit_bytes=VMEM_LIMIT_BYTES),
        name="norm_inproj",
    )(x2d, g.reshape(1, d), w_bf16)


def _attn_kernel(lamqk_ref, q_ref, k_ref, v_ref, sg_ref, o_ref, vt_ref, *, lam0):
    seq = q_ref.shape[0]
    tq = Q_TILE
    half = tq // 2

    vt_ref[...] = v_ref[...].astype(F32).T.astype(BF16)

    lq = lamqk_ref[...]
    lam = (jnp.exp(jnp.sum(lq[0:1] * lq[1:2], axis=-1, keepdims=True))
           - jnp.exp(jnp.sum(lq[2:3] * lq[3:4], axis=-1, keepdims=True)) + lam0)

    def scores_t(kb, qc):
        return lax.dot_general(kb, qc, (((1,), (1,)), ((), ())), preferred_element_type=F32)

    def fold8(x, op):
        return op(x.reshape(-1, 8, x.shape[-1]), axis=0)

    def upper_cols(wide, narrow, op):
        return jnp.concatenate([wide[:, :half], op(wide[:, half:], narrow)], axis=1)

    keep_wide = (lax.broadcasted_iota(jnp.int32, (half, tq), 0)
                 <= lax.broadcasted_iota(jnp.int32, (half, tq), 1))
    keep_narrow = keep_wide[:, :half]

    def score_stage(t, c):
        q0 = t * tq
        q = q_ref[q0:q0 + tq, :]
        lane = lax.broadcasted_iota(jnp.int32, q.shape, 1)
        mine = (lane < DIFF_HEAD_DIM) if c == 0 else (lane >= DIFF_HEAD_DIM)
        qc = jnp.where(mine, q, jnp.zeros_like(q))
        s_diag = jnp.where(keep_wide, scores_t(k_ref[q0:q0 + half, :], qc), -jnp.inf)
        s_last = jnp.where(keep_narrow, scores_t(k_ref[q0 + half:q0 + tq, :], qc[half:]), -jnp.inf)
        mpart = fold8(s_diag, jnp.max)
        s_full = None
        if q0:
            s_full = scores_t(k_ref[0:q0, :], qc)
            mpart = jnp.maximum(mpart, fold8(s_full, jnp.max))
        mpart = upper_cols(mpart, fold8(s_last, jnp.max), jnp.maximum)
        return s_full, s_diag, s_last, jnp.max(mpart, axis=0, keepdims=True)

    def value_stage(t, scores):
        q0 = t * tq
        s_full, s_diag, s_last, m = scores
        p_diag = jnp.exp2(s_diag - m)
        p_last = jnp.exp2(s_last - m[:, half:])
        lpart = fold8(p_diag, jnp.sum)
        p_wide = p_diag.astype(BF16)
        if s_full is not None:
            p_full = jnp.exp2(s_full - m)
            lpart = lpart + fold8(p_full, jnp.sum)
            p_wide = jnp.concatenate([p_full.astype(BF16), p_wide], axis=0)
        lpart = upper_cols(lpart, fold8(p_last, jnp.sum), jnp.add)
        acc = jnp.dot(vt_ref[:, 0:q0 + half], p_wide, preferred_element_type=F32)
        acc = upper_cols(acc, jnp.dot(vt_ref[:, q0 + half:q0 + tq], p_last.astype(BF16),
                                      preferred_element_type=F32), jnp.add)
        return acc, jnp.sum(lpart, axis=0, keepdims=True)

    def finish_tile(t, comp1, comp2):
        (acc1, l1), (acc2, l2) = comp1, comp2
        o = acc1 * (1.0 / l1) - acc2 * (lam / l2)
        o = o * lax.rsqrt(jnp.mean(o * o, axis=0, keepdims=True) + NORM_EPS)
        o_ref[t * tq:(t + 1) * tq, :] = (o.T * sg_ref[...] * (1.0 - lam0)).astype(BF16)

    units = [(t, c) for t in range(seq // tq) for c in range(2)]
    scores = score_stage(*units[0])
    done = {}
    for i, (t, c) in enumerate(units):
        nxt = score_stage(*units[i + 1]) if i + 1 < len(units) else None
        done[c] = value_stage(t, scores)
        scores = nxt
        if c == 1:
            finish_tile(t, done[0], done[1])


def _diff_attention(proj, lam_qk, subln_g, *, batch, seq, lam0):
    n = batch * seq
    col_blocks = N_DIFF_HEADS
    return pl.pallas_call(
        functools.partial(_attn_kernel, lam0=lam0),
        grid=(batch, N_DIFF_HEADS),
        in_specs=[
            _resident((4, DIFF_HEAD_DIM)),
            pl.BlockSpec((seq, DIFF_V_DIM), lambda b, h: (b, h)),
            pl.BlockSpec((seq, DIFF_V_DIM), lambda b, h: (b, col_blocks + h)),
            pl.BlockSpec((seq, DIFF_V_DIM), lambda b, h: (b, 2 * col_blocks + h)),
            _resident((1, DIFF_V_DIM)),
        ],
        out_specs=pl.BlockSpec((seq, DIFF_V_DIM), lambda b, h: (b, h)),
        out_shape=jax.ShapeDtypeStruct((n, N_DIFF_HEADS * DIFF_V_DIM), BF16),
        scratch_shapes=[pltpu.VMEM((DIFF_V_DIM, seq), BF16)],
        compiler_params=pltpu.CompilerParams(
            dimension_semantics=("arbitrary", "arbitrary"), vmem_limit_bytes=VMEM_LIMIT_BYTES),
        name="diff_attention",
    )(lam_qk, proj, proj, proj, subln_g.reshape(1, DIFF_V_DIM))


def _mix_ffn_kernel(x_ref, a_ref, u_ref, halo_ref, pw_ref, ps_ref, wo_ref, g2_ref,
                    wg_ref, wu_ref, wd_ref, gf_ref, o_ref, *, tiles_per_seq, final_norm):
    tm = x_ref.shape[0]
    t0 = (pl.program_id(0) % tiles_per_seq) * tm

    attn_width = a_ref.shape[1]
    d_model = x_ref.shape[1]
    col_step = d_model // len(POOL_WINDOWS)
    a_parts = []

    u = u_ref[...].astype(F32)
    halo = halo_ref[...].astype(F32)
    ext = jnp.concatenate([jnp.where(t0 == 0, jnp.zeros_like(halo), halo), u], axis=0)
    pos = t0 + lax.broadcasted_iota(jnp.int32, (tm, 1), 0)
    ys = []
    for g, w in enumerate(POOL_WINDOWS):
        a_parts.append(jnp.dot(a_ref[...], wo_ref[:attn_width, g * col_step:(g + 1) * col_step],
                               preferred_element_type=F32))
        cols = slice(g * POOL_GROUP_DIM, (g + 1) * POOL_GROUP_DIM)
        wsum = ext[:, cols]
        span = 1
        while span < w:
            wsum = wsum + pltpu.roll(wsum, shift=span, axis=0)
            span *= 2
        inv_count = 1.0 / jnp.minimum(pos + 1, w).astype(F32)
        resid = wsum[MAX_WINDOW:] * inv_count - u[:, cols]
        ys.append(jnp.dot(resid.astype(BF16), pw_ref[g], preferred_element_type=F32))
    y = jnp.concatenate(ys, axis=-1) * ps_ref[...]
    x1 = (x_ref[...] + jnp.concatenate(a_parts, axis=-1)
          + jnp.dot(y.astype(BF16), wo_ref[attn_width:, :], preferred_element_type=F32))

    h2 = (x1 * g2_ref[...]).astype(BF16)
    scale = _rms_scale(x1)
    ffn = None
    f0 = 0
    for fc in FF_CHUNKS:
        gate = jnp.dot(h2, wg_ref[:, f0:f0 + fc], preferred_element_type=F32) * scale
        up = jnp.dot(h2, wu_ref[:, f0:f0 + fc], preferred_element_type=F32) * scale
        act = (gate * (1.0 / (1.0 + jnp.exp(-gate))) * up).astype(BF16)
        part = jnp.dot(act, wd_ref[f0:f0 + fc, :], preferred_element_type=F32)
        ffn = part if ffn is None else ffn + part
        f0 += fc
    x2 = x1 + ffn
    if final_norm:
        x2 = x2 * _rms_scale(x2) * gf_ref[...]
    o_ref[...] = x2


def _mix_ffn(x2d, attn, proj, pool_w, pool_scale, w_out, g2, w_gate, w_up, w_down, final_g,
             *, seq, final_norm):
    n, d = x2d.shape
    d_ff = w_gate.shape[1]
    assert sum(FF_CHUNKS) == d_ff
    assert all(w & (w - 1) == 0 and w <= MAX_WINDOW for w in POOL_WINDOWS), "doubling needs power-of-two windows"
    pool_width = len(POOL_WINDOWS) * POOL_GROUP_DIM
    tm = TOKEN_TILE
    u_col = proj.shape[1] // pool_width - 1
    halo_per_tile = tm // MAX_WINDOW
    return pl.pallas_call(
        functools.partial(_mix_ffn_kernel, tiles_per_seq=seq // tm, final_norm=final_norm),
        grid=(n // tm,),
        in_specs=[
            pl.BlockSpec((tm, d), lambda i: (i, 0)),
            pl.BlockSpec((tm, attn.shape[1]), lambda i: (i, 0)),
            pl.BlockSpec((tm, pool_width), lambda i: (i, u_col)),
            pl.BlockSpec((MAX_WINDOW, pool_width),
                         lambda i: (jnp.maximum(i * halo_per_tile - 1, 0), u_col)),
            _resident(pool_w.shape),
            _resident((1, pool_width)),
            _resident(w_out.shape),
            _resident((1, d)),
            _resident(w_gate.shape),
            _resident(w_up.shape),
            _resident(w_down.shape),
            _resident((1, d)),
        ],
        out_specs=pl.BlockSpec((tm, d), lambda i: (i, 0)),
        out_shape=jax.ShapeDtypeStruct((n, d), F32),
        compiler_params=pltpu.CompilerParams(
            dimension_semantics=("arbitrary",), vmem_limit_bytes=VMEM_LIMIT_BYTES),
        name="mix_ffn",
    )(x2d, attn, proj, proj, pool_w, pool_scale.reshape(1, pool_width), w_out,
      g2.reshape(1, d), w_gate, w_up, w_down, final_g.reshape(1, d))


def kernel(x, norm1_g, w_in, lam_qk, subln_g, pool_w, pool_scale, w_out, norm2_g, w_gate, w_up,
           w_down, final_g):
    batch, seq, d = x.shape
    depth = w_in.shape[0]
    assert seq % TOKEN_TILE == 0 and seq % Q_TILE == 0
    x2d = x.reshape(batch * seq, d)
    for layer in range(depth):
        proj = _norm_inproj(x2d, norm1_g[layer], w_in[layer].astype(BF16))
        attn = _diff_attention(proj, lam_qk[layer], subln_g[layer], batch=batch, seq=seq,
                               lam0=_lambda_init(layer))
        x2d = _mix_ffn(x2d, attn, proj, pool_w[layer].astype(BF16), pool_scale[layer],
                       w_out[layer].astype(BF16), norm2_g[layer], w_gate[layer].astype(BF16),
                       w_up[layer].astype(BF16), w_down[layer].astype(BF16), final_g,
                       seq=seq, final_norm=(layer == depth - 1))
    return x2d.reshape(batch, seq, d)
```

```python
import functools
import math

import jax
import jax.numpy as jnp
from jax import lax
from jax.experimental import pallas as pl
from jax.experimental.pallas import tpu as pltpu

F32 = jnp.float32
BF16 = jnp.bfloat16

N_DIFF_HEADS = 4
DIFF_HEAD_DIM = 64
DIFF_V_DIM = 2 * DIFF_HEAD_DIM
ATTN_WIDTH = N_DIFF_HEADS * DIFF_V_DIM
Q_SCALE = DIFF_HEAD_DIM ** -0.5 * math.log2(math.e)
POOL_WINDOWS = (2, 4, 8, 16)
POOL_GROUP_DIM = 128
MAX_WINDOW = max(POOL_WINDOWS)
NORM_EPS = 1e-5

BF16_SUBLANE_TILE = 16

VMEM_LIMIT_BYTES = 56 * 1024 * 1024

TOKEN_TILE = 512
Q_TILE = 512
FF_CHUNKS = (1536, 1280)


def _lambda_init(layer):
    return 0.8 - 0.6 * math.exp(-0.3 * layer)


def _rms_scale(x):
    return lax.rsqrt(jnp.mean(x * x, axis=-1, keepdims=True) + NORM_EPS)


def _resident(shape):
    return pl.BlockSpec(shape, lambda *_: (0,) * len(shape), pipeline_mode=pl.Buffered(1))


def _norm_inproj_kernel(x_ref, g_ref, w_ref, *rest, n_cast):
    cast_in, o_ref, cast_out, wbf_ref = rest[:n_cast], rest[n_cast], rest[n_cast + 1:-1], rest[-1]

    @pl.when(pl.program_id(0) == 0)
    def _():
        wbf_ref[...] = w_ref[...].astype(BF16)

    for src, dst in zip(cast_in, cast_out):
        dst[...] = src[...].astype(BF16)

    x = x_ref[...]
    proj = jnp.dot((x * g_ref[...]).astype(BF16), wbf_ref[...], preferred_element_type=F32)
    scale = _rms_scale(x)
    o_ref[:, :ATTN_WIDTH] = (proj[:, :ATTN_WIDTH] * (scale * Q_SCALE)).astype(BF16)
    o_ref[:, ATTN_WIDTH:] = (proj[:, ATTN_WIDTH:] * scale).astype(BF16)


def _slab_specs(rows, cols, n_steps, layer):
    hold = next(h for h in range(1, n_steps + 1)
                if n_steps % h == 0 and rows % (n_steps // h) == 0
                and (rows // (n_steps // h)) % BF16_SUBLANE_TILE == 0)
    slab = rows // (n_steps // hold)
    return (pl.BlockSpec((None, slab, cols), lambda i: (layer, i // hold, 0)),
            pl.BlockSpec((slab, cols), lambda i: (i // hold, 0)))


def _norm_inproj(x2d, g, w_in, later_weights, layer):
    n, d = x2d.shape
    e = w_in.shape[2]
    n_steps = n // TOKEN_TILE
    specs = [_slab_specs(w.shape[1], w.shape[2], n_steps, layer) for w in later_weights]
    outs = pl.pallas_call(
        functools.partial(_norm_inproj_kernel, n_cast=len(later_weights)),
        grid=(n_steps,),
        in_specs=[
            pl.BlockSpec((TOKEN_TILE, d), lambda i: (i, 0)),
            _resident((1, d)),
            pl.BlockSpec((None, d, e), lambda i: (layer, 0, 0), pipeline_mode=pl.Buffered(1)),
            *(spec_in for spec_in, _ in specs),
        ],
        out_specs=[pl.BlockSpec((TOKEN_TILE, e), lambda i: (i, 0)),
                   *(spec_out for _, spec_out in specs)],
        out_shape=[jax.ShapeDtypeStruct((n, e), BF16),
                   *(jax.ShapeDtypeStruct(w.shape[1:], BF16) for w in later_weights)],
        scratch_shapes=[pltpu.VMEM((d, e), BF16)],
        compiler_params=pltpu.CompilerParams(
            dimension_semantics=("arbitrary",), vmem_limit_bytes=VMEM_LIMIT_BYTES),
        name="norm_inproj",
    )(x2d, g.reshape(1, d), w_in, *later_weights)
    return outs[0], outs[1:]


def _attn_kernel(lamqk_ref, q_ref, k_ref, v_ref, sg_ref, o_ref, vt_ref, *, lam0):
    seq = q_ref.shape[0]
    tq = Q_TILE
    half = tq // 2

    vt_ref[...] = v_ref[...].astype(F32).T.astype(BF16)

    lq = lamqk_ref[...]
    lam = (jnp.exp(jnp.sum(lq[0:1] * lq[1:2], axis=-1, keepdims=True))
           - jnp.exp(jnp.sum(lq[2:3] * lq[3:4], axis=-1, keepdims=True)) + lam0)

    def scores_t(kb, qc):
        return lax.dot_general(kb, qc, (((1,), (1,)), ((), ())), preferred_element_type=F32)

    def fold8(x, op):
        return op(x.reshape(-1, 8, x.shape[-1]), axis=0)

    def upper_cols(wide, narrow, op):
        return jnp.concatenate([wide[:, :half], op(wide[:, half:], narrow)], axis=1)

    keep_wide = (lax.broadcasted_iota(jnp.int32, (half, tq), 0)
                 <= lax.broadcasted_iota(jnp.int32, (half, tq), 1))
    keep_narrow = keep_wide[:, :half]

    def score_pieces(t, c, out):
        q0 = t * tq
        q = q_ref[q0:q0 + tq, :]
        lane = lax.broadcasted_iota(jnp.int32, q.shape, 1)
        mine = (lane < DIFF_HEAD_DIM) if c == 0 else (lane >= DIFF_HEAD_DIM)
        qc = jnp.where(mine, q, jnp.zeros_like(q))
        full, mpart = [], None
        for k0 in range(0, q0, tq):
            s = scores_t(k_ref[k0:k0 + tq, :], qc)
            mblk = fold8(s, jnp.max)
            mpart = mblk if mpart is None else jnp.maximum(mpart, mblk)
            full.append((k0, s))
            yield
        s_diag = jnp.where(keep_wide, scores_t(k_ref[q0:q0 + half, :], qc), -jnp.inf)
        s_last = jnp.where(keep_narrow, scores_t(k_ref[q0 + half:q0 + tq, :], qc[half:]), -jnp.inf)
        mblk = fold8(s_diag, jnp.max)
        mpart = mblk if mpart is None else jnp.maximum(mpart, mblk)
        mpart = upper_cols(mpart, fold8(s_last, jnp.max), jnp.maximum)
        out.update(full=full, s_diag=s_diag, s_last=s_last,
                   m=jnp.max(mpart, axis=0, keepdims=True))
        yield

    def value_pieces(t, scores, out):
        q0 = t * tq
        m = scores["m"]
        acc, lpart = None, None
        for k0, s in scores["full"]:
            p = jnp.exp2(s - m)
            lblk = fold8(p, jnp.sum)
            lpart = lblk if lpart is None else lpart + lblk
            part = jnp.dot(vt_ref[:, k0:k0 + tq], p.astype(BF16), preferred_element_type=F32)
            acc = part if acc is None else acc + part
            yield
        p_diag = jnp.exp2(scores["s_diag"] - m)
        p_last = jnp.exp2(scores["s_last"] - m[:, half:])
        lblk = fold8(p_diag, jnp.sum)
        lpart = lblk if lpart is None else lpart + lblk
        lpart = upper_cols(lpart, fold8(p_last, jnp.sum), jnp.add)
        part = jnp.dot(vt_ref[:, q0:q0 + half], p_diag.astype(BF16), preferred_element_type=F32)
        acc = part if acc is None else acc + part
        acc = upper_cols(acc, jnp.dot(vt_ref[:, q0 + half:q0 + tq], p_last.astype(BF16),
                                      preferred_element_type=F32), jnp.add)
        out.update(acc=acc, l=jnp.sum(lpart, axis=0, keepdims=True))
        yield

    def finish_tile(t, comp1, comp2):
        o = comp1["acc"] * (1.0 / comp1["l"]) - comp2["acc"] * (lam / comp2["l"])
        o = o * lax.rsqrt(jnp.mean(o * o, axis=0, keepdims=True) + NORM_EPS)
        o_ref[t * tq:(t + 1) * tq, :] = (o.T * sg_ref[...] * (1.0 - lam0)).astype(BF16)

    def emit_alternating(first, second):
        pending = [first, second]
        while pending:
            for g in list(pending):
                if next(g, StopIteration) is StopIteration:
                    pending.remove(g)

    units = [(t, c) for t in range(seq // tq) for c in range(2)]
    scores = {}
    emit_alternating(score_pieces(*units[0], scores), iter(()))
    done = {}
    for i, (t, c) in enumerate(units):
        nxt = {}
        done[c] = {}
        ahead = score_pieces(*units[i + 1], nxt) if i + 1 < len(units) else iter(())
        emit_alternating(ahead, value_pieces(t, scores, done[c]))
        scores = nxt
        if c == 1:
            finish_tile(t, done[0], done[1])


def _diff_attention(proj, lam_qk, subln_g, *, batch, seq, lam0):
    n = batch * seq
    col_blocks = N_DIFF_HEADS
    return pl.pallas_call(
        functools.partial(_attn_kernel, lam0=lam0),
        grid=(batch, N_DIFF_HEADS),
        in_specs=[
            _resident((4, DIFF_HEAD_DIM)),
            pl.BlockSpec((seq, DIFF_V_DIM), lambda b, h: (b, h)),
            pl.BlockSpec((seq, DIFF_V_DIM), lambda b, h: (b, col_blocks + h)),
            pl.BlockSpec((seq, DIFF_V_DIM), lambda b, h: (b, 2 * col_blocks + h)),
            _resident((1, DIFF_V_DIM)),
        ],
        out_specs=pl.BlockSpec((seq, DIFF_V_DIM), lambda b, h: (b, h)),
        out_shape=jax.ShapeDtypeStruct((n, N_DIFF_HEADS * DIFF_V_DIM), BF16),
        scratch_shapes=[pltpu.VMEM((DIFF_V_DIM, seq), BF16)],
        compiler_params=pltpu.CompilerParams(
            dimension_semantics=("arbitrary", "arbitrary"), vmem_limit_bytes=VMEM_LIMIT_BYTES),
        name="diff_attention",
    )(lam_qk, proj, proj, proj, subln_g.reshape(1, DIFF_V_DIM))


def _mix_ffn_kernel(x_ref, a_ref, u_ref, halo_ref, pw_ref, ps_ref, wo_ref, g2_ref,
                    wg_ref, wu_ref, wd_ref, gf_ref, o_ref, *, tiles_per_seq, final_norm):
    tm = x_ref.shape[0]
    t0 = (pl.program_id(0) % tiles_per_seq) * tm

    attn_width = a_ref.shape[1]
    d_model = x_ref.shape[1]
    col_step = d_model // len(POOL_WINDOWS)
    a_parts = []

    u = u_ref[...].astype(F32)
    halo = halo_ref[...].astype(F32)
    ext = jnp.concatenate([jnp.where(t0 == 0, jnp.zeros_like(halo), halo), u], axis=0)
    pos = t0 + lax.broadcasted_iota(jnp.int32, (tm, 1), 0)
    ys = []
    for g, w in enumerate(POOL_WINDOWS):
        a_parts.append(jnp.dot(a_ref[...], wo_ref[:attn_width, g * col_step:(g + 1) * col_step],
                               preferred_element_type=F32))
        cols = slice(g * POOL_GROUP_DIM, (g + 1) * POOL_GROUP_DIM)
        wsum = ext[:, cols]
        span = 1
        while span < w:
            wsum = wsum + pltpu.roll(wsum, shift=span, axis=0)
            span *= 2
        inv_count = 1.0 / jnp.minimum(pos + 1, w).astype(F32)
        resid = wsum[MAX_WINDOW:] * inv_count - u[:, cols]
        ys.append(jnp.dot(resid.astype(BF16), pw_ref[g], preferred_element_type=F32))
    y = jnp.concatenate(ys, axis=-1) * ps_ref[...]
    x1 = (x_ref[...] + jnp.concatenate(a_parts, axis=-1)
          + jnp.dot(y.astype(BF16), wo_ref[attn_width:, :], preferred_element_type=F32))

    h2 = (x1 * g2_ref[...]).astype(BF16)
    scale = _rms_scale(x1)
    ffn = None
    f0 = 0
    for fc in FF_CHUNKS:
        gate = jnp.dot(h2, wg_ref[:, f0:f0 + fc], preferred_element_type=F32) * scale
        up = jnp.dot(h2, wu_ref[:, f0:f0 + fc], preferred_element_type=F32) * scale
        act = (gate * (1.0 / (1.0 + jnp.exp(-gate))) * up).astype(BF16)
        part = jnp.dot(act, wd_ref[f0:f0 + fc, :], preferred_element_type=F32)
        ffn = part if ffn is None else ffn + part
        f0 += fc
    x2 = x1 + ffn
    if final_norm:
        x2 = x2 * _rms_scale(x2) * gf_ref[...]
    o_ref[...] = x2


def _mix_ffn(x2d, attn, proj, pool_w, pool_scale, w_out, g2, w_gate, w_up, w_down, final_g,
             *, seq, final_norm):
    n, d = x2d.shape
    d_ff = w_gate.shape[1]
    assert sum(FF_CHUNKS) == d_ff
    assert all(w & (w - 1) == 0 and w <= MAX_WINDOW for w in POOL_WINDOWS), "doubling needs power-of-two windows"
    pool_width = len(POOL_WINDOWS) * POOL_GROUP_DIM
    tm = TOKEN_TILE
    u_col = proj.shape[1] // pool_width - 1
    halo_per_tile = tm // MAX_WINDOW
    return pl.pallas_call(
        functools.partial(_mix_ffn_kernel, tiles_per_seq=seq // tm, final_norm=final_norm),
        grid=(n // tm,),
        in_specs=[
            pl.BlockSpec((tm, d), lambda i: (i, 0)),
            pl.BlockSpec((tm, attn.shape[1]), lambda i: (i, 0)),
            pl.BlockSpec((tm, pool_width), lambda i: (i, u_col)),
            pl.BlockSpec((MAX_WINDOW, pool_width),
                         lambda i: (jnp.maximum(i * halo_per_tile - 1, 0), u_col)),
            _resident(pool_w.shape),
            _resident((1, pool_width)),
            _resident(w_out.shape),
            _resident((1, d)),
            _resident(w_gate.shape),
            _resident(w_up.shape),
            _resident(w_down.shape),
            _resident((1, d)),
        ],
        out_specs=pl.BlockSpec((tm, d), lambda i: (i, 0)),
        out_shape=jax.ShapeDtypeStruct((n, d), F32),
        compiler_params=pltpu.CompilerParams(
            dimension_semantics=("arbitrary",), vmem_limit_bytes=VMEM_LIMIT_BYTES),
        name="mix_ffn",
    )(x2d, attn, proj, proj, pool_w, pool_scale.reshape(1, pool_width), w_out,
      g2.reshape(1, d), w_gate, w_up, w_down, final_g.reshape(1, d))


def kernel(x, norm1_g, w_in, lam_qk, subln_g, pool_w, pool_scale, w_out, norm2_g, w_gate, w_up,
           w_down, final_g):
    batch, seq, d = x.shape
    depth = w_in.shape[0]
    assert seq % TOKEN_TILE == 0 and seq % Q_TILE == 0
    x2d = x.reshape(batch * seq, d)
    pool_shape = pool_w.shape[1:]
    later = (pool_w.reshape(depth, -1, pool_shape[-1]), w_out, w_gate, w_up, w_down)
    for layer in range(depth):
        proj, (pw, wo, wg, wu, wd) = _norm_inproj(x2d, norm1_g[layer], w_in, later, layer)
        attn = _diff_attention(proj, lam_qk[layer], subln_g[layer], batch=batch, seq=seq,
                               lam0=_lambda_init(layer))
        x2d = _mix_ffn(x2d, attn, proj, pw.reshape(pool_shape), pool_scale[layer], wo,
                       norm2_g[layer], wg, wu, wd, final_g, seq=seq,
                       final_norm=(layer == depth - 1))
    return x2d.reshape(batch, seq, d)
```

```python
import functools
import math

import jax
import jax.numpy as jnp
from jax import lax
from jax.experimental import pallas as pl
from jax.experimental.pallas import tpu as pltpu

F32 = jnp.float32
BF16 = jnp.bfloat16

N_DIFF_HEADS = 4
DIFF_HEAD_DIM = 64
DIFF_V_DIM = 2 * DIFF_HEAD_DIM
ATTN_WIDTH = N_DIFF_HEADS * DIFF_V_DIM
Q_SCALE = DIFF_HEAD_DIM ** -0.5 * math.log2(math.e)
POOL_WINDOWS = (2, 4, 8, 16)
POOL_GROUP_DIM = 128
MAX_WINDOW = max(POOL_WINDOWS)
NORM_EPS = 1e-5

BF16_SUBLANE_TILE = 16

VMEM_LIMIT_BYTES = 56 * 1024 * 1024

TOKEN_TILE = 512
Q_TILE = 512
FF_CHUNKS = (1536, 1280)


def _lambda_init(layer):
    return 0.8 - 0.6 * math.exp(-0.3 * layer)


def _rms_scale(x):
    return lax.rsqrt(jnp.mean(x * x, axis=-1, keepdims=True) + NORM_EPS)


def _resident(shape):
    return pl.BlockSpec(shape, lambda *_: (0,) * len(shape), pipeline_mode=pl.Buffered(1))


def _norm_inproj_kernel(x_ref, g_ref, w_ref, o_ref, wbf_ref):
    @pl.when(pl.program_id(0) == 0)
    def _():
        wbf_ref[...] = w_ref[...].astype(BF16)

    x = x_ref[...]
    proj = jnp.dot((x * g_ref[...]).astype(BF16), wbf_ref[...], preferred_element_type=F32)
    scale = _rms_scale(x)
    o_ref[:, :ATTN_WIDTH] = (proj[:, :ATTN_WIDTH] * (scale * Q_SCALE)).astype(BF16)
    o_ref[:, ATTN_WIDTH:] = (proj[:, ATTN_WIDTH:] * scale).astype(BF16)


def _norm_inproj(x2d, g, w_in, layer):
    n, d = x2d.shape
    e = w_in.shape[2]
    return pl.pallas_call(
        _norm_inproj_kernel,
        grid=(n // TOKEN_TILE,),
        in_specs=[
            pl.BlockSpec((TOKEN_TILE, d), lambda i: (i, 0)),
            _resident((1, d)),
            pl.BlockSpec((None, d, e), lambda i: (layer, 0, 0), pipeline_mode=pl.Buffered(1)),
        ],
        out_specs=pl.BlockSpec((TOKEN_TILE, e), lambda i: (i, 0)),
        out_shape=jax.ShapeDtypeStruct((n, e), BF16),
        scratch_shapes=[pltpu.VMEM((d, e), BF16)],
        compiler_params=pltpu.CompilerParams(
            dimension_semantics=("arbitrary",), vmem_limit_bytes=VMEM_LIMIT_BYTES),
        name="norm_inproj",
    )(x2d, g.reshape(1, d), w_in)


def _slab_specs(rows, cols, n_steps, layer, step_of):
    hold = next(h for h in range(1, n_steps + 1)
                if n_steps % h == 0 and rows % (n_steps // h) == 0
                and (rows // (n_steps // h)) % BF16_SUBLANE_TILE == 0)
    slab = rows // (n_steps // hold)
    return (pl.BlockSpec((None, slab, cols), lambda *ids: (layer, step_of(*ids) // hold, 0)),
            pl.BlockSpec((slab, cols), lambda *ids: (step_of(*ids) // hold, 0)))


def _attn_kernel(lamqk_ref, q_ref, k_ref, v_ref, sg_ref, *rest, lam0, n_cast):
    cast_in, o_ref, cast_out, vt_ref = rest[:n_cast], rest[n_cast], rest[n_cast + 1:-1], rest[-1]
    for src, dst in zip(cast_in, cast_out):
        dst[...] = src[...].astype(BF16)

    seq = q_ref.shape[0]
    tq = Q_TILE
    half = tq // 2

    vt_ref[...] = v_ref[...].astype(F32).T.astype(BF16)

    lq = lamqk_ref[...]
    lam = (jnp.exp(jnp.sum(lq[0:1] * lq[1:2], axis=-1, keepdims=True))
           - jnp.exp(jnp.sum(lq[2:3] * lq[3:4], axis=-1, keepdims=True)) + lam0)

    def scores_t(kb, qc):
        return lax.dot_general(kb, qc, (((1,), (1,)), ((), ())), preferred_element_type=F32)

    def fold8(x, op):
        return op(x.reshape(-1, 8, x.shape[-1]), axis=0)

    def upper_cols(wide, narrow, op):
        return jnp.concatenate([wide[:, :half], op(wide[:, half:], narrow)], axis=1)

    keep_wide = (lax.broadcasted_iota(jnp.int32, (half, tq), 0)
                 <= lax.broadcasted_iota(jnp.int32, (half, tq), 1))
    keep_narrow = keep_wide[:, :half]

    def score_pieces(t, c, out):
        q0 = t * tq
        q = q_ref[q0:q0 + tq, :]
        lane = lax.broadcasted_iota(jnp.int32, q.shape, 1)
        mine = (lane < DIFF_HEAD_DIM) if c == 0 else (lane >= DIFF_HEAD_DIM)
        qc = jnp.where(mine, q, jnp.zeros_like(q))
        full, mpart = [], None
        for k0 in range(0, q0, tq):
            s = scores_t(k_ref[k0:k0 + tq, :], qc)
            mblk = fold8(s, jnp.max)
            mpart = mblk if mpart is None else jnp.maximum(mpart, mblk)
            full.append((k0, s))
            yield
        s_diag = jnp.where(keep_wide, scores_t(k_ref[q0:q0 + half, :], qc), -jnp.inf)
        s_last = jnp.where(keep_narrow, scores_t(k_ref[q0 + half:q0 + tq, :], qc[half:]), -jnp.inf)
        mblk = fold8(s_diag, jnp.max)
        mpart = mblk if mpart is None else jnp.maximum(mpart, mblk)
        mpart = upper_cols(mpart, fold8(s_last, jnp.max), jnp.maximum)
        out.update(full=full, s_diag=s_diag, s_last=s_last,
                   m=jnp.max(mpart, axis=0, keepdims=True))
        yield

    def value_pieces(t, scores, out):
        q0 = t * tq
        m = scores["m"]
        acc, lpart = None, None
        for k0, s in scores["full"]:
            p = jnp.exp2(s - m)
            lblk = fold8(p, jnp.sum)
            lpart = lblk if lpart is None else lpart + lblk
            part = jnp.dot(vt_ref[:, k0:k0 + tq], p.astype(BF16), preferred_element_type=F32)
            acc = part if acc is None else acc + part
            yield
        p_diag = jnp.exp2(scores["s_diag"] - m)
        p_last = jnp.exp2(scores["s_last"] - m[:, half:])
        lblk = fold8(p_diag, jnp.sum)
        lpart = lblk if lpart is None else lpart + lblk
        lpart = upper_cols(lpart, fold8(p_last, jnp.sum), jnp.add)
        part = jnp.dot(vt_ref[:, q0:q0 + half], p_diag.astype(BF16), preferred_element_type=F32)
        acc = part if acc is None else acc + part
        acc = upper_cols(acc, jnp.dot(vt_ref[:, q0 + half:q0 + tq], p_last.astype(BF16),
                                      preferred_element_type=F32), jnp.add)
        out.update(acc=acc, l=jnp.sum(lpart, axis=0, keepdims=True))
        yield

    def finish_tile(t, comp1, comp2):
        o = comp1["acc"] * (1.0 / comp1["l"]) - comp2["acc"] * (lam / comp2["l"])
        o = o * lax.rsqrt(jnp.mean(o * o, axis=0, keepdims=True) + NORM_EPS)
        o_ref[t * tq:(t + 1) * tq, :] = (o.T * sg_ref[...] * (1.0 - lam0)).astype(BF16)

    def emit_alternating(first, second):
        pending = [first, second]
        while pending:
            for g in list(pending):
                if next(g, StopIteration) is StopIteration:
                    pending.remove(g)

    units = [(t, c) for t in range(seq // tq) for c in range(2)]
    scores = {}
    emit_alternating(score_pieces(*units[0], scores), iter(()))
    done = {}
    for i, (t, c) in enumerate(units):
        nxt = {}
        done[c] = {}
        ahead = score_pieces(*units[i + 1], nxt) if i + 1 < len(units) else iter(())
        emit_alternating(ahead, value_pieces(t, scores, done[c]))
        scores = nxt
        if c == 1:
            finish_tile(t, done[0], done[1])


def _diff_attention(proj, lam_qk, subln_g, later_weights, layer, *, batch, seq, lam0):
    n = batch * seq
    col_blocks = N_DIFF_HEADS
    specs = [_slab_specs(w.shape[1], w.shape[2], batch * N_DIFF_HEADS, layer,
                         lambda b, h: b * N_DIFF_HEADS + h) for w in later_weights]
    outs = pl.pallas_call(
        functools.partial(_attn_kernel, lam0=lam0, n_cast=len(later_weights)),
        grid=(batch, N_DIFF_HEADS),
        in_specs=[
            _resident((4, DIFF_HEAD_DIM)),
            pl.BlockSpec((seq, DIFF_V_DIM), lambda b, h: (b, h)),
            pl.BlockSpec((seq, DIFF_V_DIM), lambda b, h: (b, col_blocks + h)),
            pl.BlockSpec((seq, DIFF_V_DIM), lambda b, h: (b, 2 * col_blocks + h)),
            _resident((1, DIFF_V_DIM)),
            *(spec_in for spec_in, _ in specs),
        ],
        out_specs=[pl.BlockSpec((seq, DIFF_V_DIM), lambda b, h: (b, h)),
                   *(spec_out for _, spec_out in specs)],
        out_shape=[jax.ShapeDtypeStruct((n, N_DIFF_HEADS * DIFF_V_DIM), BF16),
                   *(jax.ShapeDtypeStruct(w.shape[1:], BF16) for w in later_weights)],
        scratch_shapes=[pltpu.VMEM((DIFF_V_DIM, seq), BF16)],
        compiler_params=pltpu.CompilerParams(
            dimension_semantics=("arbitrary", "arbitrary"), vmem_limit_bytes=VMEM_LIMIT_BYTES),
        name="diff_attention",
    )(lam_qk, proj, proj, proj, subln_g.reshape(1, DIFF_V_DIM), *later_weights)
    return outs[0], outs[1:]


def _mix_ffn_kernel(x_ref, a_ref, u_ref, halo_ref, pw_ref, ps_ref, wo_ref, g2_ref,
                    wg_ref, wu_ref, wd_ref, gf_ref, o_ref, *, tiles_per_seq, final_norm):
    tm = x_ref.shape[0]
    t0 = (pl.program_id(0) % tiles_per_seq) * tm

    attn_width = a_ref.shape[1]
    d_model = x_ref.shape[1]
    col_step = d_model // len(POOL_WINDOWS)
    a_parts = []

    u = u_ref[...].astype(F32)
    halo = halo_ref[...].astype(F32)
    ext = jnp.concatenate([jnp.where(t0 == 0, jnp.zeros_like(halo), halo), u], axis=0)
    pos = t0 + lax.broadcasted_iota(jnp.int32, (tm, 1), 0)
    ys = []
    for g, w in enumerate(POOL_WINDOWS):
        a_parts.append(jnp.dot(a_ref[...], wo_ref[:attn_width, g * col_step:(g + 1) * col_step],
                               preferred_element_type=F32))
        cols = slice(g * POOL_GROUP_DIM, (g + 1) * POOL_GROUP_DIM)
        wsum = ext[:, cols]
        span = 1
        while span < w:
            wsum = wsum + pltpu.roll(wsum, shift=span, axis=0)
            span *= 2
        inv_count = 1.0 / jnp.minimum(pos + 1, w).astype(F32)
        resid = wsum[MAX_WINDOW:] * inv_count - u[:, cols]
        ys.append(jnp.dot(resid.astype(BF16), pw_ref[g], preferred_element_type=F32))
    y = jnp.concatenate(ys, axis=-1) * ps_ref[...]
    x1 = (x_ref[...] + jnp.concatenate(a_parts, axis=-1)
          + jnp.dot(y.astype(BF16), wo_ref[attn_width:, :], preferred_element_type=F32))

    h2 = (x1 * g2_ref[...]).astype(BF16)
    scale = _rms_scale(x1)
    ffn = None
    f0 = 0
    for fc in FF_CHUNKS:
        gate = jnp.dot(h2, wg_ref[:, f0:f0 + fc], preferred_element_type=F32) * scale
        up = jnp.dot(h2, wu_ref[:, f0:f0 + fc], preferred_element_type=F32) * scale
        act = (gate * (1.0 / (1.0 + jnp.exp(-gate))) * up).astype(BF16)
        part = jnp.dot(act, wd_ref[f0:f0 + fc, :], preferred_element_type=F32)
        ffn = part if ffn is None else ffn + part
        f0 += fc
    x2 = x1 + ffn
    if final_norm:
        x2 = x2 * _rms_scale(x2) * gf_ref[...]
    o_ref[...] = x2


def _mix_ffn(x2d, attn, proj, pool_w, pool_scale, w_out, g2, w_gate, w_up, w_down, final_g,
             *, seq, final_norm):
    n, d = x2d.shape
    d_ff = w_gate.shape[1]
    assert sum(FF_CHUNKS) == d_ff
    assert all(w & (w - 1) == 0 and w <= MAX_WINDOW for w in POOL_WINDOWS), "doubling needs power-of-two windows"
    pool_width = len(POOL_WINDOWS) * POOL_GROUP_DIM
    tm = TOKEN_TILE
    u_col = proj.shape[1] // pool_width - 1
    halo_per_tile = tm // MAX_WINDOW
    return pl.pallas_call(
        functools.partial(_mix_ffn_kernel, tiles_per_seq=seq // tm, final_norm=final_norm),
        grid=(n // tm,),
        in_specs=[
            pl.BlockSpec((tm, d), lambda i: (i, 0)),
            pl.BlockSpec((tm, attn.shape[1]), lambda i: (i, 0)),
            pl.BlockSpec((tm, pool_width), lambda i: (i, u_col)),
            pl.BlockSpec((MAX_WINDOW, pool_width),
                         lambda i: (jnp.maximum(i * halo_per_tile - 1, 0), u_col)),
            _resident(pool_w.shape),
            _resident((1, pool_width)),
            _resident(w_out.shape),
            _resident((1, d)),
            _resident(w_gate.shape),
            _resident(w_up.shape),
            _resident(w_down.shape),
            _resident((1, d)),
        ],
        out_specs=pl.BlockSpec((tm, d), lambda i: (i, 0)),
        out_shape=jax.ShapeDtypeStruct((n, d), F32),
        compiler_params=pltpu.CompilerParams(
            dimension_semantics=("arbitrary",), vmem_limit_bytes=VMEM_LIMIT_BYTES),
        name="mix_ffn",
    )(x2d, attn, proj, proj, pool_w, pool_scale.reshape(1, pool_width), w_out,
      g2.reshape(1, d), w_gate, w_up, w_down, final_g.reshape(1, d))


def kernel(x, norm1_g, w_in, lam_qk, subln_g, pool_w, pool_scale, w_out, norm2_g, w_gate, w_up,
           w_down, final_g):
    batch, seq, d = x.shape
    depth = w_in.shape[0]
    assert seq % TOKEN_TILE == 0 and seq % Q_TILE == 0
    x2d = x.reshape(batch * seq, d)
    pool_shape = pool_w.shape[1:]
    later = (pool_w.reshape(depth, -1, pool_shape[-1]), w_out, w_gate, w_up, w_down)
    for layer in range(depth):
        proj = _norm_inproj(x2d, norm1_g[layer], w_in, layer)
        attn, (pw, wo, wg, wu, wd) = _diff_attention(
            proj, lam_qk[layer], subln_g[layer], later, layer, batch=batch, seq=seq,
            lam0=_lambda_init(layer))
        x2d = _mix_ffn(x2d, attn, proj, pw.reshape(pool_shape), pool_scale[layer], wo,
                       norm2_g[layer], wg, wu, wd, final_g, seq=seq,
                       final_norm=(layer == depth - 1))
    return x2d.reshape(batch, seq, d)
```

```python
import functools
import math

import jax
import jax.numpy as jnp
from jax import lax
from jax.experimental import pallas as pl
from jax.experimental.pallas import tpu as pltpu

F32 = jnp.float32
BF16 = jnp.bfloat16

N_DIFF_HEADS = 4
DIFF_HEAD_DIM = 64
DIFF_V_DIM = 2 * DIFF_HEAD_DIM
ATTN_WIDTH = N_DIFF_HEADS * DIFF_V_DIM
Q_SCALE = DIFF_HEAD_DIM ** -0.5 * math.log2(math.e)
POOL_WINDOWS = (2, 4, 8, 16)
POOL_GROUP_DIM = 128
MAX_WINDOW = max(POOL_WINDOWS)
NORM_EPS = 1e-5

BF16_SUBLANE_TILE = 16

VMEM_LIMIT_BYTES = 56 * 1024 * 1024

TOKEN_TILE = 512
Q_TILE = 512
HEADS_PER_STEP = 2
FF_CHUNKS = (1536, 1280)


def _lambda_init(layer):
    return 0.8 - 0.6 * math.exp(-0.3 * layer)


def _rms_scale(x):
    return lax.rsqrt(jnp.mean(x * x, axis=-1, keepdims=True) + NORM_EPS)


def _resident(shape):
    return pl.BlockSpec(shape, lambda *_: (0,) * len(shape), pipeline_mode=pl.Buffered(1))


def _norm_inproj_kernel(x_ref, g_ref, w_ref, o_ref, wbf_ref):
    @pl.when(pl.program_id(0) == 0)
    def _():
        wbf_ref[...] = w_ref[...].astype(BF16)

    x = x_ref[...]
    proj = jnp.dot((x * g_ref[...]).astype(BF16), wbf_ref[...], preferred_element_type=F32)
    scale = _rms_scale(x)
    o_ref[:, :ATTN_WIDTH] = (proj[:, :ATTN_WIDTH] * (scale * Q_SCALE)).astype(BF16)
    o_ref[:, ATTN_WIDTH:] = (proj[:, ATTN_WIDTH:] * scale).astype(BF16)


def _norm_inproj(x2d, g, w_in, layer):
    n, d = x2d.shape
    e = w_in.shape[2]
    return pl.pallas_call(
        _norm_inproj_kernel,
        grid=(n // TOKEN_TILE,),
        in_specs=[
            pl.BlockSpec((TOKEN_TILE, d), lambda i: (i, 0)),
            _resident((1, d)),
            pl.BlockSpec((None, d, e), lambda i: (layer, 0, 0), pipeline_mode=pl.Buffered(1)),
        ],
        out_specs=pl.BlockSpec((TOKEN_TILE, e), lambda i: (i, 0)),
        out_shape=jax.ShapeDtypeStruct((n, e), BF16),
        scratch_shapes=[pltpu.VMEM((d, e), BF16)],
        compiler_params=pltpu.CompilerParams(
            dimension_semantics=("arbitrary",), vmem_limit_bytes=VMEM_LIMIT_BYTES),
        name="norm_inproj",
    )(x2d, g.reshape(1, d), w_in)


def _slab_specs(rows, cols, n_steps, layer, step_of):
    hold = next(h for h in range(1, n_steps + 1)
                if n_steps % h == 0 and rows % (n_steps // h) == 0
                and (rows // (n_steps // h)) % BF16_SUBLANE_TILE == 0)
    slab = rows // (n_steps // hold)
    return (pl.BlockSpec((None, slab, cols), lambda *ids: (layer, step_of(*ids) // hold, 0)),
            pl.BlockSpec((slab, cols), lambda *ids: (step_of(*ids) // hold, 0)))


def _attn_kernel(lamqk_ref, q_ref, k_ref, v_ref, sg_ref, *rest, lam0, n_cast):
    cast_in, o_ref, cast_out, vt_ref = rest[:n_cast], rest[n_cast], rest[n_cast + 1:-1], rest[-1]
    for src, dst in zip(cast_in, cast_out):
        dst[...] = src[...].astype(BF16)

    seq = q_ref.shape[0]
    tq = Q_TILE
    half = tq // 2

    def head_cols(hd):
        return slice(hd * DIFF_V_DIM, (hd + 1) * DIFF_V_DIM)

    for hd in range(HEADS_PER_STEP):
        vt_ref[hd] = v_ref[:, head_cols(hd)].astype(F32).T.astype(BF16)

    lq = lamqk_ref[...]
    lam = (jnp.exp(jnp.sum(lq[0:1] * lq[1:2], axis=-1, keepdims=True))
           - jnp.exp(jnp.sum(lq[2:3] * lq[3:4], axis=-1, keepdims=True)) + lam0)

    def scores_t(kb, qc):
        return lax.dot_general(kb, qc, (((1,), (1,)), ((), ())), preferred_element_type=F32)

    def fold8(x, op):
        return op(x.reshape(-1, 8, x.shape[-1]), axis=0)

    def upper_cols(wide, narrow, op):
        return jnp.concatenate([wide[:, :half], op(wide[:, half:], narrow)], axis=1)

    keep_wide = (lax.broadcasted_iota(jnp.int32, (half, tq), 0)
                 <= lax.broadcasted_iota(jnp.int32, (half, tq), 1))
    keep_narrow = keep_wide[:, :half]

    def score_pieces(hd, t, c, out):
        q0 = t * tq
        q = q_ref[q0:q0 + tq, head_cols(hd)]
        lane = lax.broadcasted_iota(jnp.int32, q.shape, 1)
        mine = (lane < DIFF_HEAD_DIM) if c == 0 else (lane >= DIFF_HEAD_DIM)
        qc = jnp.where(mine, q, jnp.zeros_like(q))
        full, mpart = [], None
        for k0 in range(0, q0, tq):
            s = scores_t(k_ref[k0:k0 + tq, head_cols(hd)], qc)
            mblk = fold8(s, jnp.max)
            mpart = mblk if mpart is None else jnp.maximum(mpart, mblk)
            full.append((k0, s))
            yield
        s_diag = jnp.where(keep_wide, scores_t(k_ref[q0:q0 + half, head_cols(hd)], qc), -jnp.inf)
        s_last = jnp.where(keep_narrow,
                           scores_t(k_ref[q0 + half:q0 + tq, head_cols(hd)], qc[half:]), -jnp.inf)
        mblk = fold8(s_diag, jnp.max)
        mpart = mblk if mpart is None else jnp.maximum(mpart, mblk)
        mpart = upper_cols(mpart, fold8(s_last, jnp.max), jnp.maximum)
        out.update(full=full, s_diag=s_diag, s_last=s_last,
                   m=jnp.max(mpart, axis=0, keepdims=True))
        yield

    def value_pieces(hd, t, scores, out):
        q0 = t * tq
        m = scores["m"]
        acc, lpart = None, None
        for k0, s in scores["full"]:
            p = jnp.exp2(s - m)
            lblk = fold8(p, jnp.sum)
            lpart = lblk if lpart is None else lpart + lblk
            part = jnp.dot(vt_ref[hd, :, k0:k0 + tq], p.astype(BF16), preferred_element_type=F32)
            acc = part if acc is None else acc + part
            yield
        p_diag = jnp.exp2(scores["s_diag"] - m)
        p_last = jnp.exp2(scores["s_last"] - m[:, half:])
        lblk = fold8(p_diag, jnp.sum)
        lpart = lblk if lpart is None else lpart + lblk
        lpart = upper_cols(lpart, fold8(p_last, jnp.sum), jnp.add)
        part = jnp.dot(vt_ref[hd, :, q0:q0 + half], p_diag.astype(BF16), preferred_element_type=F32)
        acc = part if acc is None else acc + part
        acc = upper_cols(acc, jnp.dot(vt_ref[hd, :, q0 + half:q0 + tq], p_last.astype(BF16),
                                      preferred_element_type=F32), jnp.add)
        out.update(acc=acc, l=jnp.sum(lpart, axis=0, keepdims=True))
        yield

    def finish_tile(hd, t, comp1, comp2):
        o = comp1["acc"] * (1.0 / comp1["l"]) - comp2["acc"] * (lam / comp2["l"])
        o = o * lax.rsqrt(jnp.mean(o * o, axis=0, keepdims=True) + NORM_EPS)
        o_ref[t * tq:(t + 1) * tq, head_cols(hd)] = (o.T * sg_ref[...] * (1.0 - lam0)).astype(BF16)

    def emit_alternating(first, second):
        pending = [first, second]
        while pending:
            for g in list(pending):
                if next(g, StopIteration) is StopIteration:
                    pending.remove(g)

    units = [(hd, t, c) for hd in range(HEADS_PER_STEP) for t in range(seq // tq) for c in range(2)]
    scores = {}
    emit_alternating(score_pieces(*units[0], scores), iter(()))
    done = {}
    for i, (hd, t, c) in enumerate(units):
        nxt = {}
        done[c] = {}
        ahead = score_pieces(*units[i + 1], nxt) if i + 1 < len(units) else iter(())
        emit_alternating(ahead, value_pieces(hd, t, scores, done[c]))
        scores = nxt
        if c == 1:
            finish_tile(hd, t, done[0], done[1])


def _diff_attention(proj, lam_qk, subln_g, later_weights, layer, *, batch, seq, lam0):
    n = batch * seq
    groups = N_DIFF_HEADS // HEADS_PER_STEP
    width = HEADS_PER_STEP * DIFF_V_DIM
    specs = [_slab_specs(w.shape[1], w.shape[2], batch * groups, layer,
                         lambda b, h: b * groups + h) for w in later_weights]
    outs = pl.pallas_call(
        functools.partial(_attn_kernel, lam0=lam0, n_cast=len(later_weights)),
        grid=(batch, groups),
        in_specs=[
            _resident((4, DIFF_HEAD_DIM)),
            pl.BlockSpec((seq, width), lambda b, h: (b, h)),
            pl.BlockSpec((seq, width), lambda b, h: (b, groups + h)),
            pl.BlockSpec((seq, width), lambda b, h: (b, 2 * groups + h)),
            _resident((1, DIFF_V_DIM)),
            *(spec_in for spec_in, _ in specs),
        ],
        out_specs=[pl.BlockSpec((seq, width), lambda b, h: (b, h)),
                   *(spec_out for _, spec_out in specs)],
        out_shape=[jax.ShapeDtypeStruct((n, N_DIFF_HEADS * DIFF_V_DIM), BF16),
                   *(jax.ShapeDtypeStruct(w.shape[1:], BF16) for w in later_weights)],
        scratch_shapes=[pltpu.VMEM((HEADS_PER_STEP, DIFF_V_DIM, seq), BF16)],
        compiler_params=pltpu.CompilerParams(
            dimension_semantics=("arbitrary", "arbitrary"), vmem_limit_bytes=VMEM_LIMIT_BYTES),
        name="diff_attention",
    )(lam_qk, proj, proj, proj, subln_g.reshape(1, DIFF_V_DIM), *later_weights)
    return outs[0], outs[1:]


def _mix_ffn_kernel(x_ref, a_ref, u_ref, halo_ref, pw_ref, ps_ref, wo_ref, g2_ref,
                    wg_ref, wu_ref, wd_ref, gf_ref, o_ref, *, tiles_per_seq, final_norm):
    tm = x_ref.shape[0]
    t0 = (pl.program_id(0) % tiles_per_seq) * tm

    attn_width = a_ref.shape[1]
    d_model = x_ref.shape[1]
    col_step = d_model // len(POOL_WINDOWS)
    a_parts = []

    u = u_ref[...].astype(F32)
    halo = halo_ref[...].astype(F32)
    ext = jnp.concatenate([jnp.where(t0 == 0, jnp.zeros_like(halo), halo), u], axis=0)
    pos = t0 + lax.broadcasted_iota(jnp.int32, (tm, 1), 0)
    ys = []
    for g, w in enumerate(POOL_WINDOWS):
        a_parts.append(jnp.dot(a_ref[...], wo_ref[:attn_width, g * col_step:(g + 1) * col_step],
                               preferred_element_type=F32))
        cols = slice(g * POOL_GROUP_DIM, (g + 1) * POOL_GROUP_DIM)
        wsum = ext[:, cols]
        span = 1
        while span < w:
            wsum = wsum + pltpu.roll(wsum, shift=span, axis=0)
            span *= 2
        inv_count = 1.0 / jnp.minimum(pos + 1, w).astype(F32)
        resid = wsum[MAX_WINDOW:] * inv_count - u[:, cols]
        ys.append(jnp.dot(resid.astype(BF16), pw_ref[g], preferred_element_type=F32))
    y = jnp.concatenate(ys, axis=-1) * ps_ref[...]
    x1 = (x_ref[...] + jnp.concatenate(a_parts, axis=-1)
          + jnp.dot(y.astype(BF16), wo_ref[attn_width:, :], preferred_element_type=F32))

    h2 = (x1 * g2_ref[...]).astype(BF16)
    scale = _rms_scale(x1)
    ffn = None
    f0 = 0
    for fc in FF_CHUNKS:
        gate = jnp.dot(h2, wg_ref[:, f0:f0 + fc], preferred_element_type=F32) * scale
        up = jnp.dot(h2, wu_ref[:, f0:f0 + fc], preferred_element_type=F32) * scale
        act = (gate * (1.0 / (1.0 + jnp.exp(-gate))) * up).astype(BF16)
        part = jnp.dot(act, wd_ref[f0:f0 + fc, :], preferred_element_type=F32)
        ffn = part if ffn is None else ffn + part
        f0 += fc
    x2 = x1 + ffn
    if final_norm:
        x2 = x2 * _rms_scale(x2) * gf_ref[...]
    o_ref[...] = x2


def _mix_ffn(x2d, attn, proj, pool_w, pool_scale, w_out, g2, w_gate, w_up, w_down, final_g,
             *, seq, final_norm):
    n, d = x2d.shape
    d_ff = w_gate.shape[1]
    assert sum(FF_CHUNKS) == d_ff
    assert all(w & (w - 1) == 0 and w <= MAX_WINDOW for w in POOL_WINDOWS), "doubling needs power-of-two windows"
    pool_width = len(POOL_WINDOWS) * POOL_GROUP_DIM
    tm = TOKEN_TILE
    u_col = proj.shape[1] // pool_width - 1
    halo_per_tile = tm // MAX_WINDOW
    return pl.pallas_call(
        functools.partial(_mix_ffn_kernel, tiles_per_seq=seq // tm, final_norm=final_norm),
        grid=(n // tm,),
        in_specs=[
            pl.BlockSpec((tm, d), lambda i: (i, 0)),
            pl.BlockSpec((tm, attn.shape[1]), lambda i: (i, 0)),
            pl.BlockSpec((tm, pool_width), lambda i: (i, u_col)),
            pl.BlockSpec((MAX_WINDOW, pool_width),
                         lambda i: (jnp.maximum(i * halo_per_tile - 1, 0), u_col)),
            _resident(pool_w.shape),
            _resident((1, pool_width)),
            _resident(w_out.shape),
            _resident((1, d)),
            _resident(w_gate.shape),
            _resident(w_up.shape),
            _resident(w_down.shape),
            _resident((1, d)),
        ],
        out_specs=pl.BlockSpec((tm, d), lambda i: (i, 0)),
        out_shape=jax.ShapeDtypeStruct((n, d), F32),
        compiler_params=pltpu.CompilerParams(
            dimension_semantics=("arbitrary",), vmem_limit_bytes=VMEM_LIMIT_BYTES),
        name="mix_ffn",
    )(x2d, attn, proj, proj, pool_w, pool_scale.reshape(1, pool_width), w_out,
      g2.reshape(1, d), w_gate, w_up, w_down, final_g.reshape(1, d))


def kernel(x, norm1_g, w_in, lam_qk, subln_g, pool_w, pool_scale, w_out, norm2_g, w_gate, w_up,
           w_down, final_g):
    batch, seq, d = x.shape
    depth = w_in.shape[0]
    assert seq % TOKEN_TILE == 0 and seq % Q_TILE == 0
    x2d = x.reshape(batch * seq, d)
    pool_shape = pool_w.shape[1:]
    later = (pool_w.reshape(depth, -1, pool_shape[-1]), w_out, w_gate, w_up, w_down)
    for layer in range(depth):
        proj = _norm_inproj(x2d, norm1_g[layer], w_in, layer)
        attn, (pw, wo, wg, wu, wd) = _diff_attention(
            proj, lam_qk[layer], subln_g[layer], later, layer, batch=batch, seq=seq,
            lam0=_lambda_init(layer))
        x2d = _mix_ffn(x2d, attn, proj, pw.reshape(pool_shape), pool_scale[layer], wo,
                       norm2_g[layer], wg, wu, wd, final_g, seq=seq,
                       final_norm=(layer == depth - 1))
    return x2d.reshape(batch, seq, d)
```

```python
import functools
import math

import jax
import jax.numpy as jnp
from jax import lax
from jax.experimental import pallas as pl
from jax.experimental.pallas import tpu as pltpu

F32 = jnp.float32
BF16 = jnp.bfloat16

N_DIFF_HEADS = 4
DIFF_HEAD_DIM = 64
DIFF_V_DIM = 2 * DIFF_HEAD_DIM
ATTN_WIDTH = N_DIFF_HEADS * DIFF_V_DIM
Q_SCALE = DIFF_HEAD_DIM ** -0.5 * math.log2(math.e)
POOL_WINDOWS = (2, 4, 8, 16)
POOL_GROUP_DIM = 128
MAX_WINDOW = max(POOL_WINDOWS)
NORM_EPS = 1e-5

BF16_SUBLANE_TILE = 16

VMEM_LIMIT_BYTES = 56 * 1024 * 1024

TOKEN_TILE = 512
INPROJ_TILE = 1024
Q_TILE = 512
HEADS_PER_STEP = 2
FF_CHUNKS = (1536, 1280)


def _lambda_init(layer):
    return 0.8 - 0.6 * math.exp(-0.3 * layer)


def _rms_scale(x):
    return lax.rsqrt(jnp.mean(x * x, axis=-1, keepdims=True) + NORM_EPS)


def _resident(shape):
    return pl.BlockSpec(shape, lambda *_: (0,) * len(shape), pipeline_mode=pl.Buffered(1))


def _norm_inproj_kernel(x_ref, g_ref, w_ref, o_ref, wbf_ref):
    @pl.when(pl.program_id(0) == 0)
    def _():
        wbf_ref[...] = w_ref[...].astype(BF16)

    x = x_ref[...]
    proj = jnp.dot((x * g_ref[...]).astype(BF16), wbf_ref[...], preferred_element_type=F32)
    scale = _rms_scale(x)
    o_ref[:, :ATTN_WIDTH] = (proj[:, :ATTN_WIDTH] * (scale * Q_SCALE)).astype(BF16)
    o_ref[:, ATTN_WIDTH:] = (proj[:, ATTN_WIDTH:] * scale).astype(BF16)


def _norm_inproj(x2d, g, w_in, layer):
    n, d = x2d.shape
    e = w_in.shape[2]
    return pl.pallas_call(
        _norm_inproj_kernel,
        grid=(n // INPROJ_TILE,),
        in_specs=[
            pl.BlockSpec((INPROJ_TILE, d), lambda i: (i, 0)),
            _resident((1, d)),
            pl.BlockSpec((None, d, e), lambda i: (layer, 0, 0), pipeline_mode=pl.Buffered(1)),
        ],
        out_specs=pl.BlockSpec((INPROJ_TILE, e), lambda i: (i, 0)),
        out_shape=jax.ShapeDtypeStruct((n, e), BF16),
        scratch_shapes=[pltpu.VMEM((d, e), BF16)],
        compiler_params=pltpu.CompilerParams(
            dimension_semantics=("arbitrary",), vmem_limit_bytes=VMEM_LIMIT_BYTES),
        name="norm_inproj",
    )(x2d, g.reshape(1, d), w_in)


def _slab_specs(rows, cols, n_steps, layer, step_of):
    hold = next(h for h in range(1, n_steps + 1)
                if n_steps % h == 0 and rows % (n_steps // h) == 0
                and (rows // (n_steps // h)) % BF16_SUBLANE_TILE == 0)
    slab = rows // (n_steps // hold)
    return (pl.BlockSpec((None, slab, cols), lambda *ids: (layer, step_of(*ids) // hold, 0)),
            pl.BlockSpec((slab, cols), lambda *ids: (step_of(*ids) // hold, 0)))


def _attn_kernel(lamqk_ref, q_ref, k_ref, v_ref, sg_ref, *rest, lam0, n_cast):
    cast_in, o_ref, cast_out, vt_ref = rest[:n_cast], rest[n_cast], rest[n_cast + 1:-1], rest[-1]
    for src, dst in zip(cast_in, cast_out):
        dst[...] = src[...].astype(BF16)

    seq = q_ref.shape[0]
    tq = Q_TILE
    half = tq // 2

    def head_cols(hd):
        return slice(hd * DIFF_V_DIM, (hd + 1) * DIFF_V_DIM)

    for hd in range(HEADS_PER_STEP):
        vt_ref[hd] = v_ref[:, head_cols(hd)].astype(F32).T.astype(BF16)

    lq = lamqk_ref[...]
    lam = (jnp.exp(jnp.sum(lq[0:1] * lq[1:2], axis=-1, keepdims=True))
           - jnp.exp(jnp.sum(lq[2:3] * lq[3:4], axis=-1, keepdims=True)) + lam0)

    def scores_t(kb, qc):
        return lax.dot_general(kb, qc, (((1,), (1,)), ((), ())), preferred_element_type=F32)

    def fold8(x, op):
        return op(x.reshape(-1, 8, x.shape[-1]), axis=0)

    def upper_cols(wide, narrow, op):
        return jnp.concatenate([wide[:, :half], op(wide[:, half:], narrow)], axis=1)

    keep_wide = (lax.broadcasted_iota(jnp.int32, (half, tq), 0)
                 <= lax.broadcasted_iota(jnp.int32, (half, tq), 1))
    keep_narrow = keep_wide[:, :half]

    def score_pieces(hd, t, c, out):
        q0 = t * tq
        q = q_ref[q0:q0 + tq, head_cols(hd)]
        lane = lax.broadcasted_iota(jnp.int32, q.shape, 1)
        mine = (lane < DIFF_HEAD_DIM) if c == 0 else (lane >= DIFF_HEAD_DIM)
        qc = jnp.where(mine, q, jnp.zeros_like(q))
        full, mpart = [], None
        for k0 in range(0, q0, tq):
            s = scores_t(k_ref[k0:k0 + tq, head_cols(hd)], qc)
            mblk = fold8(s, jnp.max)
            mpart = mblk if mpart is None else jnp.maximum(mpart, mblk)
            full.append((k0, s))
            yield
        s_diag = jnp.where(keep_wide, scores_t(k_ref[q0:q0 + half, head_cols(hd)], qc), -jnp.inf)
        s_last = jnp.where(keep_narrow,
                           scores_t(k_ref[q0 + half:q0 + tq, head_cols(hd)], qc[half:]), -jnp.inf)
        mblk = fold8(s_diag, jnp.max)
        mpart = mblk if mpart is None else jnp.maximum(mpart, mblk)
        mpart = upper_cols(mpart, fold8(s_last, jnp.max), jnp.maximum)
        out.update(full=full, s_diag=s_diag, s_last=s_last,
                   m=jnp.max(mpart, axis=0, keepdims=True))
        yield

    def value_pieces(hd, t, scores, out):
        q0 = t * tq
        m = scores["m"]
        acc, lpart = None, None
        for k0, s in scores["full"]:
            p = jnp.exp2(s - m)
            lblk = fold8(p, jnp.sum)
            lpart = lblk if lpart is None else lpart + lblk
            part = jnp.dot(vt_ref[hd, :, k0:k0 + tq], p.astype(BF16), preferred_element_type=F32)
            acc = part if acc is None else acc + part
            yield
        p_diag = jnp.exp2(scores["s_diag"] - m)
        p_last = jnp.exp2(scores["s_last"] - m[:, half:])
        lblk = fold8(p_diag, jnp.sum)
        lpart = lblk if lpart is None else lpart + lblk
        lpart = upper_cols(lpart, fold8(p_last, jnp.sum), jnp.add)
        part = jnp.dot(vt_ref[hd, :, q0:q0 + half], p_diag.astype(BF16), preferred_element_type=F32)
        acc = part if acc is None else acc + part
        acc = upper_cols(acc, jnp.dot(vt_ref[hd, :, q0 + half:q0 + tq], p_last.astype(BF16),
                                      preferred_element_type=F32), jnp.add)
        out.update(acc=acc, l=jnp.sum(lpart, axis=0, keepdims=True))
        yield

    def finish_tile(hd, t, comp1, comp2):
        o = comp1["acc"] * (1.0 / comp1["l"]) - comp2["acc"] * (lam / comp2["l"])
        o = o * lax.rsqrt(jnp.mean(o * o, axis=0, keepdims=True) + NORM_EPS)
        o_ref[t * tq:(t + 1) * tq, head_cols(hd)] = (o.T * sg_ref[...] * (1.0 - lam0)).astype(BF16)

    def emit_alternating(first, second):
        pending = [first, second]
        while pending:
            for g in list(pending):
                if next(g, StopIteration) is StopIteration:
                    pending.remove(g)

    units = [(hd, t, c) for hd in range(HEADS_PER_STEP) for t in range(seq // tq) for c in range(2)]
    scores = {}
    emit_alternating(score_pieces(*units[0], scores), iter(()))
    done = {}
    for i, (hd, t, c) in enumerate(units):
        nxt = {}
        done[c] = {}
        ahead = score_pieces(*units[i + 1], nxt) if i + 1 < len(units) else iter(())
        emit_alternating(ahead, value_pieces(hd, t, scores, done[c]))
        scores = nxt
        if c == 1:
            finish_tile(hd, t, done[0], done[1])


def _diff_attention(proj, lam_qk, subln_g, later_weights, layer, *, batch, seq, lam0):
    n = batch * seq
    groups = N_DIFF_HEADS // HEADS_PER_STEP
    width = HEADS_PER_STEP * DIFF_V_DIM
    specs = [_slab_specs(w.shape[1], w.shape[2], batch * groups, layer,
                         lambda b, h: b * groups + h) for w in later_weights]
    outs = pl.pallas_call(
        functools.partial(_attn_kernel, lam0=lam0, n_cast=len(later_weights)),
        grid=(batch, groups),
        in_specs=[
            _resident((4, DIFF_HEAD_DIM)),
            pl.BlockSpec((seq, width), lambda b, h: (b, h)),
            pl.BlockSpec((seq, width), lambda b, h: (b, groups + h)),
            pl.BlockSpec((seq, width), lambda b, h: (b, 2 * groups + h)),
            _resident((1, DIFF_V_DIM)),
            *(spec_in for spec_in, _ in specs),
        ],
        out_specs=[pl.BlockSpec((seq, width), lambda b, h: (b, h)),
                   *(spec_out for _, spec_out in specs)],
        out_shape=[jax.ShapeDtypeStruct((n, N_DIFF_HEADS * DIFF_V_DIM), BF16),
                   *(jax.ShapeDtypeStruct(w.shape[1:], BF16) for w in later_weights)],
        scratch_shapes=[pltpu.VMEM((HEADS_PER_STEP, DIFF_V_DIM, seq), BF16)],
        compiler_params=pltpu.CompilerParams(
            dimension_semantics=("arbitrary", "arbitrary"), vmem_limit_bytes=VMEM_LIMIT_BYTES),
        name="diff_attention",
    )(lam_qk, proj, proj, proj, subln_g.reshape(1, DIFF_V_DIM), *later_weights)
    return outs[0], outs[1:]


def _mix_ffn_kernel(x_ref, a_ref, u_ref, halo_ref, pw_ref, ps_ref, wo_ref, g2_ref,
                    wg_ref, wu_ref, wd_ref, gf_ref, o_ref, wpool_ref, *, tiles_per_seq, final_norm):
    tm = x_ref.shape[0]
    t0 = (pl.program_id(0) % tiles_per_seq) * tm
    attn_width = a_ref.shape[1]

    @pl.when(pl.program_id(0) == 0)
    def _():
        for g in range(len(POOL_WINDOWS)):
            rows = slice(g * POOL_GROUP_DIM, (g + 1) * POOL_GROUP_DIM)
            scaled = (pw_ref[g].astype(F32) * ps_ref[:, rows]).astype(BF16)
            wpool_ref[rows, :] = jnp.dot(
                scaled, wo_ref[attn_width + g * POOL_GROUP_DIM:attn_width + (g + 1) * POOL_GROUP_DIM, :],
                preferred_element_type=F32).astype(BF16)

    d_model = x_ref.shape[1]
    col_step = d_model // len(POOL_WINDOWS)
    a_parts = []

    u = u_ref[...].astype(F32)
    halo = halo_ref[...].astype(F32)
    ext = jnp.concatenate([jnp.where(t0 == 0, jnp.zeros_like(halo), halo), u], axis=0)
    pos = t0 + lax.broadcasted_iota(jnp.int32, (tm, 1), 0)
    resids = []
    for g, w in enumerate(POOL_WINDOWS):
        a_parts.append(jnp.dot(a_ref[...], wo_ref[:attn_width, g * col_step:(g + 1) * col_step],
                               preferred_element_type=F32))
        cols = slice(g * POOL_GROUP_DIM, (g + 1) * POOL_GROUP_DIM)
        wsum = ext[:, cols]
        span = 1
        while span < w:
            wsum = wsum + pltpu.roll(wsum, shift=span, axis=0)
            span *= 2
        inv_count = 1.0 / jnp.minimum(pos + 1, w).astype(F32)
        resids.append((wsum[MAX_WINDOW:] * inv_count - u[:, cols]).astype(BF16))
    x1 = (x_ref[...] + jnp.concatenate(a_parts, axis=-1)
          + jnp.dot(jnp.concatenate(resids, axis=-1), wpool_ref[...], preferred_element_type=F32))

    h2 = (x1 * g2_ref[...]).astype(BF16)
    scale = _rms_scale(x1)
    ffn = None
    f0 = 0
    for fc in FF_CHUNKS:
        gate = jnp.dot(h2, wg_ref[:, f0:f0 + fc], preferred_element_type=F32) * scale
        up = jnp.dot(h2, wu_ref[:, f0:f0 + fc], preferred_element_type=F32) * scale
        act = (gate * (1.0 / (1.0 + jnp.exp(-gate))) * up).astype(BF16)
        part = jnp.dot(act, wd_ref[f0:f0 + fc, :], preferred_element_type=F32)
        ffn = part if ffn is None else ffn + part
        f0 += fc
    x2 = x1 + ffn
    if final_norm:
        x2 = x2 * _rms_scale(x2) * gf_ref[...]
    o_ref[...] = x2


def _mix_ffn(x2d, attn, proj, pool_w, pool_scale, w_out, g2, w_gate, w_up, w_down, final_g,
             *, seq, final_norm):
    n, d = x2d.shape
    d_ff = w_gate.shape[1]
    assert sum(FF_CHUNKS) == d_ff
    assert all(w & (w - 1) == 0 and w <= MAX_WINDOW for w in POOL_WINDOWS), "doubling needs power-of-two windows"
    pool_width = len(POOL_WINDOWS) * POOL_GROUP_DIM
    tm = TOKEN_TILE
    u_col = proj.shape[1] // pool_width - 1
    halo_per_tile = tm // MAX_WINDOW
    return pl.pallas_call(
        functools.partial(_mix_ffn_kernel, tiles_per_seq=seq // tm, final_norm=final_norm),
        grid=(n // tm,),
        in_specs=[
            pl.BlockSpec((tm, d), lambda i: (i, 0)),
            pl.BlockSpec((tm, attn.shape[1]), lambda i: (i, 0)),
            pl.BlockSpec((tm, pool_width), lambda i: (i, u_col)),
            pl.BlockSpec((MAX_WINDOW, pool_width),
                         lambda i: (jnp.maximum(i * halo_per_tile - 1, 0), u_col)),
            _resident(pool_w.shape),
            _resident((1, pool_width)),
            _resident(w_out.shape),
            _resident((1, d)),
            _resident(w_gate.shape),
            _resident(w_up.shape),
            _resident(w_down.shape),
            _resident((1, d)),
        ],
        out_specs=pl.BlockSpec((tm, d), lambda i: (i, 0)),
        out_shape=jax.ShapeDtypeStruct((n, d), F32),
        scratch_shapes=[pltpu.VMEM((pool_width, d), BF16)],
        compiler_params=pltpu.CompilerParams(
            dimension_semantics=("arbitrary",), vmem_limit_bytes=VMEM_LIMIT_BYTES),
        name="mix_ffn",
    )(x2d, attn, proj, proj, pool_w, pool_scale.reshape(1, pool_width), w_out,
      g2.reshape(1, d), w_gate, w_up, w_down, final_g.reshape(1, d))


def kernel(x, norm1_g, w_in, lam_qk, subln_g, pool_w, pool_scale, w_out, norm2_g, w_gate, w_up,
           w_down, final_g):
    batch, seq, d = x.shape
    depth = w_in.shape[0]
    assert seq % TOKEN_TILE == 0 and seq % Q_TILE == 0 and (batch * seq) % INPROJ_TILE == 0
    x2d = x.reshape(batch * seq, d)
    pool_shape = pool_w.shape[1:]
    later = (pool_w.reshape(depth, -1, pool_shape[-1]), w_out, w_gate, w_up, w_down)
    for layer in range(depth):
        proj = _norm_inproj(x2d, norm1_g[layer], w_in, layer)
        attn, (pw, wo, wg, wu, wd) = _diff_attention(
            proj, lam_qk[layer], subln_g[layer], later, layer, batch=batch, seq=seq,
            lam0=_lambda_init(layer))
        x2d = _mix_ffn(x2d, attn, proj, pw.reshape(pool_shape), pool_scale[layer], wo,
                       norm2_g[layer], wg, wu, wd, final_g, seq=seq,
                       final_norm=(layer == depth - 1))
    return x2d.reshape(batch, seq, d)
```

```python
import functools
import math

import jax
import jax.numpy as jnp
from jax import lax
from jax.experimental import pallas as pl
from jax.experimental.pallas import tpu as pltpu

F32 = jnp.float32
BF16 = jnp.bfloat16

N_DIFF_HEADS = 4
DIFF_HEAD_DIM = 64
DIFF_V_DIM = 2 * DIFF_HEAD_DIM
ATTN_WIDTH = N_DIFF_HEADS * DIFF_V_DIM
Q_SCALE = DIFF_HEAD_DIM ** -0.5 * math.log2(math.e)
POOL_WINDOWS = (2, 4, 8, 16)
POOL_GROUP_DIM = 128
MAX_WINDOW = max(POOL_WINDOWS)
NORM_EPS = 1e-5

BF16_SUBLANE_TILE = 16
MXU_TILE = 256

VMEM_LIMIT_BYTES = 56 * 1024 * 1024

TOKEN_TILE = 512
INPROJ_TILE = 1024
Q_TILE = 512
KEY_BLOCK = 512
HEADS_PER_STEP = 2
FF_CHUNKS = (1536, 1280)


def _lambda_init(layer):
    return 0.8 - 0.6 * math.exp(-0.3 * layer)


def _rms_scale(x):
    return lax.rsqrt(jnp.mean(x * x, axis=-1, keepdims=True) + NORM_EPS)


def _resident(shape):
    return pl.BlockSpec(shape, lambda *_: (0,) * len(shape), pipeline_mode=pl.Buffered(1))


def _norm_inproj_kernel(x_ref, g_ref, w_ref, o_ref, wbf_ref):
    @pl.when(pl.program_id(0) == 0)
    def _():
        wbf_ref[...] = w_ref[...].astype(BF16)

    x = x_ref[...]
    proj = jnp.dot((x * g_ref[...]).astype(BF16), wbf_ref[...], preferred_element_type=F32)
    scale = _rms_scale(x)
    o_ref[:, :ATTN_WIDTH] = (proj[:, :ATTN_WIDTH] * (scale * Q_SCALE)).astype(BF16)
    o_ref[:, ATTN_WIDTH:] = (proj[:, ATTN_WIDTH:] * scale).astype(BF16)


def _norm_inproj(x2d, g, w_in, layer):
    n, d = x2d.shape
    e = w_in.shape[2]
    return pl.pallas_call(
        _norm_inproj_kernel,
        grid=(n // INPROJ_TILE,),
        in_specs=[
            pl.BlockSpec((INPROJ_TILE, d), lambda i: (i, 0)),
            _resident((1, d)),
            pl.BlockSpec((None, d, e), lambda i: (layer, 0, 0), pipeline_mode=pl.Buffered(1)),
        ],
        out_specs=pl.BlockSpec((INPROJ_TILE, e), lambda i: (i, 0)),
        out_shape=jax.ShapeDtypeStruct((n, e), BF16),
        scratch_shapes=[pltpu.VMEM((d, e), BF16)],
        compiler_params=pltpu.CompilerParams(
            dimension_semantics=("arbitrary",), vmem_limit_bytes=VMEM_LIMIT_BYTES),
        name="norm_inproj",
    )(x2d, g.reshape(1, d), w_in)


def _slab_specs(rows, cols, n_steps, layer, step_of):
    hold = next(h for h in range(1, n_steps + 1)
                if n_steps % h == 0 and rows % (n_steps // h) == 0
                and (rows // (n_steps // h)) % BF16_SUBLANE_TILE == 0)
    slab = rows // (n_steps // hold)
    return (pl.BlockSpec((None, slab, cols), lambda *ids: (layer, step_of(*ids) // hold, 0)),
            pl.BlockSpec((slab, cols), lambda *ids: (step_of(*ids) // hold, 0)))


def _attn_kernel(lamqk_ref, q_ref, k_ref, v_ref, sg_ref, *rest, lam0, n_cast):
    cast_in, o_ref, cast_out, vt_ref = rest[:n_cast], rest[n_cast], rest[n_cast + 1:-1], rest[-1]
    for src, dst in zip(cast_in, cast_out):
        dst[...] = src[...].astype(BF16)

    seq = q_ref.shape[0]
    tq = Q_TILE
    half = tq // 2

    def head_cols(hd):
        return slice(hd * DIFF_V_DIM, (hd + 1) * DIFF_V_DIM)

    for hd in range(HEADS_PER_STEP):
        vt_ref[hd] = v_ref[:, head_cols(hd)].astype(F32).T.astype(BF16)

    lq = lamqk_ref[...]
    lam = (jnp.exp(jnp.sum(lq[0:1] * lq[1:2], axis=-1, keepdims=True))
           - jnp.exp(jnp.sum(lq[2:3] * lq[3:4], axis=-1, keepdims=True)) + lam0)

    def scores_t(kb, qc):
        return lax.dot_general(kb, qc, (((1,), (1,)), ((), ())), preferred_element_type=F32)

    def fold8(x, op):
        return op(x.reshape(-1, 8, x.shape[-1]), axis=0)

    def upper_cols(wide, narrow, op):
        return jnp.concatenate([wide[:, :half], op(wide[:, half:], narrow)], axis=1)

    keep_wide = (lax.broadcasted_iota(jnp.int32, (half, tq), 0)
                 <= lax.broadcasted_iota(jnp.int32, (half, tq), 1))
    keep_narrow = keep_wide[:, :half]

    def score_pieces(hd, t, c, out):
        q0 = t * tq
        q = q_ref[q0:q0 + tq, head_cols(hd)]
        lane = lax.broadcasted_iota(jnp.int32, q.shape, 1)
        mine = (lane < DIFF_HEAD_DIM) if c == 0 else (lane >= DIFF_HEAD_DIM)
        qc = jnp.where(mine, q, jnp.zeros_like(q))
        full, mpart = [], None
        for k0 in range(0, q0, KEY_BLOCK):
            k1 = min(k0 + KEY_BLOCK, q0)
            s = scores_t(k_ref[k0:k1, head_cols(hd)], qc)
            mblk = fold8(s, jnp.max)
            mpart = mblk if mpart is None else jnp.maximum(mpart, mblk)
            full.append((k0, k1, s))
            yield
        s_diag = jnp.where(keep_wide, scores_t(k_ref[q0:q0 + half, head_cols(hd)], qc), -jnp.inf)
        s_last = jnp.where(keep_narrow,
                           scores_t(k_ref[q0 + half:q0 + tq, head_cols(hd)], qc[half:]), -jnp.inf)
        mblk = fold8(s_diag, jnp.max)
        mpart = mblk if mpart is None else jnp.maximum(mpart, mblk)
        mpart = upper_cols(mpart, fold8(s_last, jnp.max), jnp.maximum)
        out.update(full=full, s_diag=s_diag, s_last=s_last,
                   m=jnp.max(mpart, axis=0, keepdims=True))
        yield

    def value_pieces(hd, t, scores, out):
        q0 = t * tq
        m = scores["m"]
        acc, lpart = None, None
        for k0, k1, s in scores["full"]:
            p = jnp.exp2(s - m)
            lblk = fold8(p, jnp.sum)
            lpart = lblk if lpart is None else lpart + lblk
            part = jnp.dot(vt_ref[hd, :, k0:k1], p.astype(BF16), preferred_element_type=F32)
            acc = part if acc is None else acc + part
            yield
        p_diag = jnp.exp2(scores["s_diag"] - m)
        p_last = jnp.exp2(scores["s_last"] - m[:, half:])
        lblk = fold8(p_diag, jnp.sum)
        lpart = lblk if lpart is None else lpart + lblk
        lpart = upper_cols(lpart, fold8(p_last, jnp.sum), jnp.add)
        part = jnp.dot(vt_ref[hd, :, q0:q0 + half], p_diag.astype(BF16), preferred_element_type=F32)
        acc = part if acc is None else acc + part
        acc = upper_cols(acc, jnp.dot(vt_ref[hd, :, q0 + half:q0 + tq], p_last.astype(BF16),
                                      preferred_element_type=F32), jnp.add)
        out.update(acc=acc, l=jnp.sum(lpart, axis=0, keepdims=True))
        yield

    def finish_tile(hd, t, comp1, comp2):
        o = comp1["acc"] * (1.0 / comp1["l"]) - comp2["acc"] * (lam / comp2["l"])
        o = o * lax.rsqrt(jnp.mean(o * o, axis=0, keepdims=True) + NORM_EPS)
        o_ref[t * tq:(t + 1) * tq, head_cols(hd)] = (o.T * sg_ref[...] * (1.0 - lam0)).astype(BF16)

    def emit_alternating(first, second):
        pending = [first, second]
        while pending:
            for g in list(pending):
                if next(g, StopIteration) is StopIteration:
                    pending.remove(g)

    units = [(hd, t, c) for hd in range(HEADS_PER_STEP) for t in range(seq // tq) for c in range(2)]
    scores = {}
    emit_alternating(score_pieces(*units[0], scores), iter(()))
    done = {}
    for i, (hd, t, c) in enumerate(units):
        nxt = {}
        done[c] = {}
        ahead = score_pieces(*units[i + 1], nxt) if i + 1 < len(units) else iter(())
        emit_alternating(ahead, value_pieces(hd, t, scores, done[c]))
        scores = nxt
        if c == 1:
            finish_tile(hd, t, done[0], done[1])


def _diff_attention(proj, lam_qk, subln_g, later_weights, layer, *, batch, seq, lam0):
    n = batch * seq
    groups = N_DIFF_HEADS // HEADS_PER_STEP
    width = HEADS_PER_STEP * DIFF_V_DIM
    specs = [_slab_specs(w.shape[1], w.shape[2], batch * groups, layer,
                         lambda b, h: b * groups + h) for w in later_weights]
    outs = pl.pallas_call(
        functools.partial(_attn_kernel, lam0=lam0, n_cast=len(later_weights)),
        grid=(batch, groups),
        in_specs=[
            _resident((4, DIFF_HEAD_DIM)),
            pl.BlockSpec((seq, width), lambda b, h: (b, h)),
            pl.BlockSpec((seq, width), lambda b, h: (b, groups + h)),
            pl.BlockSpec((seq, width), lambda b, h: (b, 2 * groups + h)),
            _resident((1, DIFF_V_DIM)),
            *(spec_in for spec_in, _ in specs),
        ],
        out_specs=[pl.BlockSpec((seq, width), lambda b, h: (b, h)),
                   *(spec_out for _, spec_out in specs)],
        out_shape=[jax.ShapeDtypeStruct((n, N_DIFF_HEADS * DIFF_V_DIM), BF16),
                   *(jax.ShapeDtypeStruct(w.shape[1:], BF16) for w in later_weights)],
        scratch_shapes=[pltpu.VMEM((HEADS_PER_STEP, DIFF_V_DIM, seq), BF16)],
        compiler_params=pltpu.CompilerParams(
            dimension_semantics=("arbitrary", "arbitrary"), vmem_limit_bytes=VMEM_LIMIT_BYTES),
        name="diff_attention",
    )(lam_qk, proj, proj, proj, subln_g.reshape(1, DIFF_V_DIM), *later_weights)
    return outs[0], outs[1:]


def _mix_ffn_kernel(x_ref, a_ref, u_ref, halo_ref, pw_ref, ps_ref, wo_ref, g2_ref,
                    wg_ref, wu_ref, wd_ref, gf_ref, o_ref, wpool_ref, *, tiles_per_seq, final_norm):
    tm = x_ref.shape[0]
    t0 = (pl.program_id(0) % tiles_per_seq) * tm
    attn_width = a_ref.shape[1]

    @pl.when(pl.program_id(0) == 0)
    def _():
        for g in range(len(POOL_WINDOWS)):
            rows = slice(g * POOL_GROUP_DIM, (g + 1) * POOL_GROUP_DIM)
            scaled = (pw_ref[g].astype(F32) * ps_ref[:, rows]).astype(BF16)
            wpool_ref[rows, :] = jnp.dot(
                scaled, wo_ref[attn_width + g * POOL_GROUP_DIM:attn_width + (g + 1) * POOL_GROUP_DIM, :],
                preferred_element_type=F32).astype(BF16)

    d_model = x_ref.shape[1]
    col_step = d_model // len(POOL_WINDOWS)
    a_parts = []

    u = u_ref[...].astype(F32)
    halo = halo_ref[...].astype(F32)
    ext = jnp.concatenate([jnp.where(t0 == 0, jnp.zeros_like(halo), halo), u], axis=0)
    pos = t0 + lax.broadcasted_iota(jnp.int32, (tm, 1), 0)
    resids = []
    for g, w in enumerate(POOL_WINDOWS):
        a_parts.append(jnp.dot(a_ref[...], wo_ref[:attn_width, g * col_step:(g + 1) * col_step],
                               preferred_element_type=F32))
        cols = slice(g * POOL_GROUP_DIM, (g + 1) * POOL_GROUP_DIM)
        wsum = ext[:, cols]
        span = 1
        while span < w:
            wsum = wsum + pltpu.roll(wsum, shift=span, axis=0)
            span *= 2
        inv_count = 1.0 / jnp.minimum(pos + 1, w).astype(F32)
        resids.append((wsum[MAX_WINDOW:] * inv_count - u[:, cols]).astype(BF16))
    x1 = (x_ref[...] + jnp.concatenate(a_parts, axis=-1)
          + jnp.dot(jnp.concatenate(resids, axis=-1), wpool_ref[...], preferred_element_type=F32))

    h2 = (x1 * g2_ref[...]).astype(BF16)
    scale = _rms_scale(x1)
    ffn = None
    f0 = 0
    for fc in FF_CHUNKS:
        acts = []
        for c0 in range(f0, f0 + fc, MXU_TILE):
            w_gu = jnp.concatenate([wg_ref[:, c0:c0 + MXU_TILE], wu_ref[:, c0:c0 + MXU_TILE]], axis=1)
            gu = jnp.dot(h2, w_gu, preferred_element_type=F32) * scale
            gate, up = gu[:, :MXU_TILE], gu[:, MXU_TILE:]
            acts.append((gate * (1.0 / (1.0 + jnp.exp(-gate))) * up).astype(BF16))
        act = jnp.concatenate(acts, axis=1)
        part = jnp.dot(act, wd_ref[f0:f0 + fc, :], preferred_element_type=F32)
        ffn = part if ffn is None else ffn + part
        f0 += fc
    x2 = x1 + ffn
    if final_norm:
        x2 = x2 * _rms_scale(x2) * gf_ref[...]
    o_ref[...] = x2


def _mix_ffn(x2d, attn, proj, pool_w, pool_scale, w_out, g2, w_gate, w_up, w_down, final_g,
             *, seq, final_norm):
    n, d = x2d.shape
    d_ff = w_gate.shape[1]
    assert sum(FF_CHUNKS) == d_ff
    assert all(w & (w - 1) == 0 and w <= MAX_WINDOW for w in POOL_WINDOWS), "doubling needs power-of-two windows"
    pool_width = len(POOL_WINDOWS) * POOL_GROUP_DIM
    tm = TOKEN_TILE
    u_col = proj.shape[1] // pool_width - 1
    halo_per_tile = tm // MAX_WINDOW
    return pl.pallas_call(
        functools.partial(_mix_ffn_kernel, tiles_per_seq=seq // tm, final_norm=final_norm),
        grid=(n // tm,),
        in_specs=[
            pl.BlockSpec((tm, d), lambda i: (i, 0)),
            pl.BlockSpec((tm, attn.shape[1]), lambda i: (i, 0)),
            pl.BlockSpec((tm, pool_width), lambda i: (i, u_col)),
            pl.BlockSpec((MAX_WINDOW, pool_width),
                         lambda i: (jnp.maximum(i * halo_per_tile - 1, 0), u_col)),
            _resident(pool_w.shape),
            _resident((1, pool_width)),
            _resident(w_out.shape),
            _resident((1, d)),
            _resident(w_gate.shape),
            _resident(w_up.shape),
            _resident(w_down.shape),
            _resident((1, d)),
        ],
        out_specs=pl.BlockSpec((tm, d), lambda i: (i, 0)),
        out_shape=jax.ShapeDtypeStruct((n, d), F32),
        scratch_shapes=[pltpu.VMEM((pool_width, d), BF16)],
        compiler_params=pltpu.CompilerParams(
            dimension_semantics=("arbitrary",), vmem_limit_bytes=VMEM_LIMIT_BYTES),
        name="mix_ffn",
    )(x2d, attn, proj, proj, pool_w, pool_scale.reshape(1, pool_width), w_out,
      g2.reshape(1, d), w_gate, w_up, w_down, final_g.reshape(1, d))


def kernel(x, norm1_g, w_in, lam_qk, subln_g, pool_w, pool_scale, w_out, norm2_g, w_gate, w_up,
           w_down, final_g):
    batch, seq, d = x.shape
    depth = w_in.shape[0]
    assert seq % TOKEN_TILE == 0 and seq % Q_TILE == 0 and (batch * seq) % INPROJ_TILE == 0
    x2d = x.reshape(batch * seq, d)
    pool_shape = pool_w.shape[1:]
    later = (pool_w.reshape(depth, -1, pool_shape[-1]), w_out, w_gate, w_up, w_down)
    for layer in range(depth):
        proj = _norm_inproj(x2d, norm1_g[layer], w_in, layer)
        attn, (pw, wo, wg, wu, wd) = _diff_attention(
            proj, lam_qk[layer], subln_g[layer], later, layer, batch=batch, seq=seq,
            lam0=_lambda_init(layer))
        x2d = _mix_ffn(x2d, attn, proj, pw.reshape(pool_shape), pool_scale[layer], wo,
                       norm2_g[layer], wg, wu, wd, final_g, seq=seq,
                       final_norm=(layer == depth - 1))
    return x2d.reshape(batch, seq, d)
```

```python
import functools
import math

import jax
import jax.numpy as jnp
from jax import lax
from jax.experimental import pallas as pl
from jax.experimental.pallas import tpu as pltpu

F32 = jnp.float32
BF16 = jnp.bfloat16

N_DIFF_HEADS = 4
DIFF_HEAD_DIM = 64
DIFF_V_DIM = 2 * DIFF_HEAD_DIM
ATTN_WIDTH = N_DIFF_HEADS * DIFF_V_DIM
Q_SCALE = DIFF_HEAD_DIM ** -0.5 * math.log2(math.e)
POOL_WINDOWS = (2, 4, 8, 16)
POOL_GROUP_DIM = 128
MAX_WINDOW = max(POOL_WINDOWS)
NORM_EPS = 1e-5

BF16_SUBLANE_TILE = 16
MXU_TILE = 256

VMEM_LIMIT_BYTES = 56 * 1024 * 1024

TOKEN_TILE = 512
MIX_TILES_PER_STEP = 2
INPROJ_TILE = 1024
Q_TILE = 512
KEY_BLOCK = 512
HEADS_PER_STEP = 2
FF_CHUNKS = (1536, 1280)


def _lambda_init(layer):
    return 0.8 - 0.6 * math.exp(-0.3 * layer)


def _rms_scale(x):
    return lax.rsqrt(jnp.mean(x * x, axis=-1, keepdims=True) + NORM_EPS)


def _resident(shape):
    return pl.BlockSpec(shape, lambda *_: (0,) * len(shape), pipeline_mode=pl.Buffered(1))


def _norm_inproj_kernel(x_ref, g_ref, w_ref, o_ref, wbf_ref):
    @pl.when(pl.program_id(0) == 0)
    def _():
        wbf_ref[...] = w_ref[...].astype(BF16)

    x = x_ref[...]
    proj = jnp.dot((x * g_ref[...]).astype(BF16), wbf_ref[...], preferred_element_type=F32)
    scale = _rms_scale(x)
    o_ref[:, :ATTN_WIDTH] = (proj[:, :ATTN_WIDTH] * (scale * Q_SCALE)).astype(BF16)
    o_ref[:, ATTN_WIDTH:] = (proj[:, ATTN_WIDTH:] * scale).astype(BF16)


def _norm_inproj(x2d, g, w_in, layer):
    n, d = x2d.shape
    e = w_in.shape[2]
    return pl.pallas_call(
        _norm_inproj_kernel,
        grid=(n // INPROJ_TILE,),
        in_specs=[
            pl.BlockSpec((INPROJ_TILE, d), lambda i: (i, 0)),
            _resident((1, d)),
            pl.BlockSpec((None, d, e), lambda i: (layer, 0, 0), pipeline_mode=pl.Buffered(1)),
        ],
        out_specs=pl.BlockSpec((INPROJ_TILE, e), lambda i: (i, 0)),
        out_shape=jax.ShapeDtypeStruct((n, e), BF16),
        scratch_shapes=[pltpu.VMEM((d, e), BF16)],
        compiler_params=pltpu.CompilerParams(
            dimension_semantics=("arbitrary",), vmem_limit_bytes=VMEM_LIMIT_BYTES),
        name="norm_inproj",
    )(x2d, g.reshape(1, d), w_in)


def _slab_specs(rows, cols, n_steps, layer, step_of):
    hold = next(h for h in range(1, n_steps + 1)
                if n_steps % h == 0 and rows % (n_steps // h) == 0
                and (rows // (n_steps // h)) % BF16_SUBLANE_TILE == 0)
    slab = rows // (n_steps // hold)
    return (pl.BlockSpec((None, slab, cols), lambda *ids: (layer, step_of(*ids) // hold, 0)),
            pl.BlockSpec((slab, cols), lambda *ids: (step_of(*ids) // hold, 0)))


def _attn_kernel(lamqk_ref, q_ref, k_ref, v_ref, sg_ref, *rest, lam0, n_cast):
    cast_in, o_ref, cast_out, vt_ref = rest[:n_cast], rest[n_cast], rest[n_cast + 1:-1], rest[-1]
    for src, dst in zip(cast_in, cast_out):
        dst[...] = src[...].astype(BF16)

    seq = q_ref.shape[0]
    tq = Q_TILE
    half = tq // 2

    def head_cols(hd):
        return slice(hd * DIFF_V_DIM, (hd + 1) * DIFF_V_DIM)

    for hd in range(HEADS_PER_STEP):
        vt_ref[hd] = v_ref[:, head_cols(hd)].astype(F32).T.astype(BF16)

    lq = lamqk_ref[...]
    lam = (jnp.exp(jnp.sum(lq[0:1] * lq[1:2], axis=-1, keepdims=True))
           - jnp.exp(jnp.sum(lq[2:3] * lq[3:4], axis=-1, keepdims=True)) + lam0)

    def scores_t(kb, qc):
        return lax.dot_general(kb, qc, (((1,), (1,)), ((), ())), preferred_element_type=F32)

    def fold8(x, op):
        return op(x.reshape(-1, 8, x.shape[-1]), axis=0)

    def upper_cols(wide, narrow, op):
        return jnp.concatenate([wide[:, :half], op(wide[:, half:], narrow)], axis=1)

    keep_wide = (lax.broadcasted_iota(jnp.int32, (half, tq), 0)
                 <= lax.broadcasted_iota(jnp.int32, (half, tq), 1))
    keep_narrow = keep_wide[:, :half]

    def score_pieces(hd, t, c, out):
        q0 = t * tq
        q = q_ref[q0:q0 + tq, head_cols(hd)]
        lane = lax.broadcasted_iota(jnp.int32, q.shape, 1)
        mine = (lane < DIFF_HEAD_DIM) if c == 0 else (lane >= DIFF_HEAD_DIM)
        qc = jnp.where(mine, q, jnp.zeros_like(q))
        full, mpart = [], None
        for k0 in range(0, q0, KEY_BLOCK):
            k1 = min(k0 + KEY_BLOCK, q0)
            s = scores_t(k_ref[k0:k1, head_cols(hd)], qc)
            mblk = fold8(s, jnp.max)
            mpart = mblk if mpart is None else jnp.maximum(mpart, mblk)
            full.append((k0, k1, s))
            yield
        s_diag = jnp.where(keep_wide, scores_t(k_ref[q0:q0 + half, head_cols(hd)], qc), -jnp.inf)
        s_last = jnp.where(keep_narrow,
                           scores_t(k_ref[q0 + half:q0 + tq, head_cols(hd)], qc[half:]), -jnp.inf)
        mblk = fold8(s_diag, jnp.max)
        mpart = mblk if mpart is None else jnp.maximum(mpart, mblk)
        mpart = upper_cols(mpart, fold8(s_last, jnp.max), jnp.maximum)
        out.update(full=full, s_diag=s_diag, s_last=s_last,
                   m=jnp.max(mpart, axis=0, keepdims=True))
        yield

    def value_pieces(hd, t, scores, out):
        q0 = t * tq
        m = scores["m"]
        acc, lpart = None, None
        for k0, k1, s in scores["full"]:
            p = jnp.exp2(s - m)
            lblk = fold8(p, jnp.sum)
            lpart = lblk if lpart is None else lpart + lblk
            part = jnp.dot(vt_ref[hd, :, k0:k1], p.astype(BF16), preferred_element_type=F32)
            acc = part if acc is None else acc + part
            yield
        p_diag = jnp.exp2(scores["s_diag"] - m)
        p_last = jnp.exp2(scores["s_last"] - m[:, half:])
        lblk = fold8(p_diag, jnp.sum)
        lpart = lblk if lpart is None else lpart + lblk
        lpart = upper_cols(lpart, fold8(p_last, jnp.sum), jnp.add)
        part = jnp.dot(vt_ref[hd, :, q0:q0 + half], p_diag.astype(BF16), preferred_element_type=F32)
        acc = part if acc is None else acc + part
        acc = upper_cols(acc, jnp.dot(vt_ref[hd, :, q0 + half:q0 + tq], p_last.astype(BF16),
                                      preferred_element_type=F32), jnp.add)
        out.update(acc=acc, l=jnp.sum(lpart, axis=0, keepdims=True))
        yield

    def finish_tile(hd, t, comp1, comp2):
        o = comp1["acc"] * (1.0 / comp1["l"]) - comp2["acc"] * (lam / comp2["l"])
        o = o * lax.rsqrt(jnp.mean(o * o, axis=0, keepdims=True) + NORM_EPS)
        o_ref[t * tq:(t + 1) * tq, head_cols(hd)] = (o.T * sg_ref[...] * (1.0 - lam0)).astype(BF16)

    def emit_alternating(first, second):
        pending = [first, second]
        while pending:
            for g in list(pending):
                if next(g, StopIteration) is StopIteration:
                    pending.remove(g)

    units = [(hd, t, c) for hd in range(HEADS_PER_STEP) for t in range(seq // tq) for c in range(2)]
    scores = {}
    emit_alternating(score_pieces(*units[0], scores), iter(()))
    done = {}
    for i, (hd, t, c) in enumerate(units):
        nxt = {}
        done[c] = {}
        ahead = score_pieces(*units[i + 1], nxt) if i + 1 < len(units) else iter(())
        emit_alternating(ahead, value_pieces(hd, t, scores, done[c]))
        scores = nxt
        if c == 1:
            finish_tile(hd, t, done[0], done[1])


def _diff_attention(proj, lam_qk, subln_g, later_weights, layer, *, batch, seq, lam0):
    n = batch * seq
    groups = N_DIFF_HEADS // HEADS_PER_STEP
    width = HEADS_PER_STEP * DIFF_V_DIM
    specs = [_slab_specs(w.shape[1], w.shape[2], batch * groups, layer,
                         lambda b, h: b * groups + h) for w in later_weights]
    outs = pl.pallas_call(
        functools.partial(_attn_kernel, lam0=lam0, n_cast=len(later_weights)),
        grid=(batch, groups),
        in_specs=[
            _resident((4, DIFF_HEAD_DIM)),
            pl.BlockSpec((seq, width), lambda b, h: (b, h)),
            pl.BlockSpec((seq, width), lambda b, h: (b, groups + h)),
            pl.BlockSpec((seq, width), lambda b, h: (b, 2 * groups + h)),
            _resident((1, DIFF_V_DIM)),
            *(spec_in for spec_in, _ in specs),
        ],
        out_specs=[pl.BlockSpec((seq, width), lambda b, h: (b, h)),
                   *(spec_out for _, spec_out in specs)],
        out_shape=[jax.ShapeDtypeStruct((n, N_DIFF_HEADS * DIFF_V_DIM), BF16),
                   *(jax.ShapeDtypeStruct(w.shape[1:], BF16) for w in later_weights)],
        scratch_shapes=[pltpu.VMEM((HEADS_PER_STEP, DIFF_V_DIM, seq), BF16)],
        compiler_params=pltpu.CompilerParams(
            dimension_semantics=("arbitrary", "arbitrary"), vmem_limit_bytes=VMEM_LIMIT_BYTES),
        name="diff_attention",
    )(lam_qk, proj, proj, proj, subln_g.reshape(1, DIFF_V_DIM), *later_weights)
    return outs[0], outs[1:]


def _mix_ffn_kernel(x_ref, a_ref, u_ref, halo_ref, pw_ref, ps_ref, wo_ref, g2_ref,
                    wg_ref, wu_ref, wd_ref, gf_ref, o_ref, wpool_ref, *, tiles_per_seq, final_norm):
    tm = TOKEN_TILE
    attn_width = a_ref.shape[1]

    @pl.when(pl.program_id(0) == 0)
    def _():
        for g in range(len(POOL_WINDOWS)):
            rows = slice(g * POOL_GROUP_DIM, (g + 1) * POOL_GROUP_DIM)
            scaled = (pw_ref[g].astype(F32) * ps_ref[:, rows]).astype(BF16)
            wpool_ref[rows, :] = jnp.dot(
                scaled, wo_ref[attn_width + g * POOL_GROUP_DIM:attn_width + (g + 1) * POOL_GROUP_DIM, :],
                preferred_element_type=F32).astype(BF16)

    def token_tile(sub, carry):
        _mix_ffn_tile(sub, x_ref, a_ref, u_ref, halo_ref, wpool_ref, wo_ref, g2_ref, wg_ref, wu_ref,
                      wd_ref, gf_ref, o_ref, tiles_per_seq=tiles_per_seq, final_norm=final_norm)
        return carry

    lax.fori_loop(0, x_ref.shape[0] // tm, token_tile, 0)


def _mix_ffn_tile(sub, x_ref, a_ref, u_ref, halo_ref, wpool_ref, wo_ref, g2_ref, wg_ref, wu_ref,
                  wd_ref, gf_ref, o_ref, *, tiles_per_seq, final_norm):
    tm = TOKEN_TILE
    tiles_per_step = x_ref.shape[0] // tm
    attn_width = a_ref.shape[1]
    row0 = pl.multiple_of(sub * tm, tm)
    rows = pl.ds(row0, tm)
    t0 = ((pl.program_id(0) * tiles_per_step + sub) % tiles_per_seq) * tm

    d_model = x_ref.shape[1]
    col_step = d_model // len(POOL_WINDOWS)
    a_parts = []

    u = u_ref[rows, :].astype(F32)
    inside = u_ref[pl.ds(pl.multiple_of(jnp.maximum(row0 - MAX_WINDOW, 0), MAX_WINDOW), MAX_WINDOW), :]
    halo = jnp.where(sub == 0, halo_ref[...], inside).astype(F32)
    ext = jnp.concatenate([jnp.where(t0 == 0, jnp.zeros_like(halo), halo), u], axis=0)
    pos = t0 + lax.broadcasted_iota(jnp.int32, (tm, 1), 0)
    resids = []
    for g, w in enumerate(POOL_WINDOWS):
        a_parts.append(jnp.dot(a_ref[rows, :], wo_ref[:attn_width, g * col_step:(g + 1) * col_step],
                               preferred_element_type=F32))
        cols = slice(g * POOL_GROUP_DIM, (g + 1) * POOL_GROUP_DIM)
        wsum = ext[:, cols]
        span = 1
        while span < w:
            wsum = wsum + pltpu.roll(wsum, shift=span, axis=0)
            span *= 2
        inv_count = 1.0 / jnp.minimum(pos + 1, w).astype(F32)
        resids.append((wsum[MAX_WINDOW:] * inv_count - u[:, cols]).astype(BF16))
    x1 = (x_ref[rows, :] + jnp.concatenate(a_parts, axis=-1)
          + jnp.dot(jnp.concatenate(resids, axis=-1), wpool_ref[...], preferred_element_type=F32))

    h2 = (x1 * g2_ref[...]).astype(BF16)
    scale = _rms_scale(x1)
    ffn = None
    f0 = 0
    for fc in FF_CHUNKS:
        acts = []
        for c0 in range(f0, f0 + fc, MXU_TILE):
            w_gu = jnp.concatenate([wg_ref[:, c0:c0 + MXU_TILE], wu_ref[:, c0:c0 + MXU_TILE]], axis=1)
            gu = jnp.dot(h2, w_gu, preferred_element_type=F32) * scale
            gate, up = gu[:, :MXU_TILE], gu[:, MXU_TILE:]
            acts.append((gate * (1.0 / (1.0 + jnp.exp(-gate))) * up).astype(BF16))
        act = jnp.concatenate(acts, axis=1)
        part = jnp.dot(act, wd_ref[f0:f0 + fc, :], preferred_element_type=F32)
        ffn = part if ffn is None else ffn + part
        f0 += fc
    x2 = x1 + ffn
    if final_norm:
        x2 = x2 * _rms_scale(x2) * gf_ref[...]
    o_ref[rows, :] = x2


def _mix_ffn(x2d, attn, proj, pool_w, pool_scale, w_out, g2, w_gate, w_up, w_down, final_g,
             *, seq, final_norm):
    n, d = x2d.shape
    d_ff = w_gate.shape[1]
    assert sum(FF_CHUNKS) == d_ff
    assert all(w & (w - 1) == 0 and w <= MAX_WINDOW for w in POOL_WINDOWS), "doubling needs power-of-two windows"
    pool_width = len(POOL_WINDOWS) * POOL_GROUP_DIM
    tm = MIX_TILES_PER_STEP * TOKEN_TILE
    u_col = proj.shape[1] // pool_width - 1
    halo_per_tile = tm // MAX_WINDOW
    return pl.pallas_call(
        functools.partial(_mix_ffn_kernel, tiles_per_seq=seq // TOKEN_TILE, final_norm=final_norm),
        grid=(n // tm,),
        in_specs=[
            pl.BlockSpec((tm, d), lambda i: (i, 0)),
            pl.BlockSpec((tm, attn.shape[1]), lambda i: (i, 0)),
            pl.BlockSpec((tm, pool_width), lambda i: (i, u_col)),
            pl.BlockSpec((MAX_WINDOW, pool_width),
                         lambda i: (jnp.maximum(i * halo_per_tile - 1, 0), u_col)),
            _resident(pool_w.shape),
            _resident((1, pool_width)),
            _resident(w_out.shape),
            _resident((1, d)),
            _resident(w_gate.shape),
            _resident(w_up.shape),
            _resident(w_down.shape),
            _resident((1, d)),
        ],
        out_specs=pl.BlockSpec((tm, d), lambda i: (i, 0)),
        out_shape=jax.ShapeDtypeStruct((n, d), F32),
        scratch_shapes=[pltpu.VMEM((pool_width, d), BF16)],
        compiler_params=pltpu.CompilerParams(
            dimension_semantics=("arbitrary",), vmem_limit_bytes=VMEM_LIMIT_BYTES),
        name="mix_ffn",
    )(x2d, attn, proj, proj, pool_w, pool_scale.reshape(1, pool_width), w_out,
      g2.reshape(1, d), w_gate, w_up, w_down, final_g.reshape(1, d))


def kernel(x, norm1_g, w_in, lam_qk, subln_g, pool_w, pool_scale, w_out, norm2_g, w_gate, w_up,
           w_down, final_g):
    batch, seq, d = x.shape
    depth = w_in.shape[0]
    assert seq % (MIX_TILES_PER_STEP * TOKEN_TILE) == 0 and seq % Q_TILE == 0
    assert (batch * seq) % INPROJ_TILE == 0
    x2d = x.reshape(batch * seq, d)
    pool_shape = pool_w.shape[1:]
    later = (pool_w.reshape(depth, -1, pool_shape[-1]), w_out, w_gate, w_up, w_down)
    for layer in range(depth):
        proj = _norm_inproj(x2d, norm1_g[layer], w_in, layer)
        attn, (pw, wo, wg, wu, wd) = _diff_attention(
            proj, lam_qk[layer], subln_g[layer], later, layer, batch=batch, seq=seq,
            lam0=_lambda_init(layer))
        x2d = _mix_ffn(x2d, attn, proj, pw.reshape(pool_shape), pool_scale[layer], wo,
                       norm2_g[layer], wg, wu, wd, final_g, seq=seq,
                       final_norm=(layer == depth - 1))
    return x2d.reshape(batch, seq, d)
```

```python
import functools
import math

import jax
import jax.numpy as jnp
from jax import lax
from jax.experimental import pallas as pl
from jax.experimental.pallas import tpu as pltpu

F32 = jnp.float32
BF16 = jnp.bfloat16

N_DIFF_HEADS = 4
DIFF_HEAD_DIM = 64
DIFF_V_DIM = 2 * DIFF_HEAD_DIM
LAMBDA_VECTORS = 4
ATTN_WIDTH = N_DIFF_HEADS * DIFF_V_DIM
Q_SCALE = DIFF_HEAD_DIM ** -0.5 * math.log2(math.e)
POOL_WINDOWS = (2, 4, 8, 16)
POOL_GROUP_DIM = 128
MAX_WINDOW = max(POOL_WINDOWS)
NORM_EPS = 1e-5

F32_SUBLANES = 8
BF16_SUBLANE_TILE = 16
MXU_TILE = 256

VMEM_LIMIT_BYTES = 56 * 1024 * 1024

TOKEN_TILE = 512
MIX_TILES_PER_STEP = 2
INPROJ_TILE = 1024
Q_TILE = 512
KEY_BLOCK = 512
HEADS_PER_STEP = 2
FF_CHUNKS = (1536, 1280)


def _lambda_init(layer):
    return 0.8 - 0.6 * math.exp(-0.3 * layer)


def _rms_scale(x):
    return lax.rsqrt(jnp.mean(x * x, axis=-1, keepdims=True) + NORM_EPS)


def _resident(shape):
    return pl.BlockSpec(shape, lambda *_: (0,) * len(shape), pipeline_mode=pl.Buffered(1))


def _norm_inproj_kernel(x_ref, g_ref, w_ref, o_ref, wbf_ref):
    @pl.when(pl.program_id(0) == 0)
    def _():
        wbf_ref[...] = w_ref[...].astype(BF16)

    x = x_ref[...]
    proj = jnp.dot((x * g_ref[...]).astype(BF16), wbf_ref[...], preferred_element_type=F32)
    scale = _rms_scale(x)
    o_ref[:, :ATTN_WIDTH] = (proj[:, :ATTN_WIDTH] * (scale * Q_SCALE)).astype(BF16)
    o_ref[:, ATTN_WIDTH:] = (proj[:, ATTN_WIDTH:] * scale).astype(BF16)


def _norm_inproj(x2d, g, w_in, layer):
    n, d = x2d.shape
    e = w_in.shape[2]
    return pl.pallas_call(
        _norm_inproj_kernel,
        grid=(n // INPROJ_TILE,),
        in_specs=[
            pl.BlockSpec((INPROJ_TILE, d), lambda i: (i, 0)),
            _resident((1, d)),
            pl.BlockSpec((None, d, e), lambda i: (layer, 0, 0), pipeline_mode=pl.Buffered(1)),
        ],
        out_specs=pl.BlockSpec((INPROJ_TILE, e), lambda i: (i, 0)),
        out_shape=jax.ShapeDtypeStruct((n, e), BF16),
        scratch_shapes=[pltpu.VMEM((d, e), BF16)],
        compiler_params=pltpu.CompilerParams(
            dimension_semantics=("arbitrary",), vmem_limit_bytes=VMEM_LIMIT_BYTES),
        name="norm_inproj",
    )(x2d, g.reshape(1, d), w_in)


def _slab_specs(rows, cols, n_steps, layer, step_of):
    hold = next(h for h in range(1, n_steps + 1)
                if n_steps % h == 0 and rows % (n_steps // h) == 0
                and (rows // (n_steps // h)) % BF16_SUBLANE_TILE == 0)
    slab = rows // (n_steps // hold)
    return (pl.BlockSpec((None, slab, cols), lambda *ids: (layer, step_of(*ids) // hold, 0)),
            pl.BlockSpec((slab, cols), lambda *ids: (step_of(*ids) // hold, 0)))


def _attn_kernel(lamqk_ref, q_ref, k_ref, v_ref, sg_ref, *rest, lam0, n_cast):
    cast_in, o_ref, cast_out, vt_ref = rest[:n_cast], rest[n_cast], rest[n_cast + 1:-1], rest[-1]
    for src, dst in zip(cast_in, cast_out):
        dst[...] = src[...].astype(BF16)

    seq = q_ref.shape[0]
    tq = Q_TILE
    half = tq // 2

    def head_cols(hd):
        return slice(hd * DIFF_V_DIM, (hd + 1) * DIFF_V_DIM)

    for hd in range(HEADS_PER_STEP):
        vt_ref[hd] = v_ref[:, head_cols(hd)].astype(F32).T.astype(BF16)

    lq = lamqk_ref[...]
    lam = (jnp.exp(jnp.sum(lq[0:1] * lq[1:2], axis=-1, keepdims=True))
           - jnp.exp(jnp.sum(lq[2:3] * lq[3:4], axis=-1, keepdims=True)) + lam0)

    def scores_t(kb, qc):
        return lax.dot_general(kb, qc, (((1,), (1,)), ((), ())), preferred_element_type=F32)

    def fold8(x, op):
        return op(x.reshape(-1, F32_SUBLANES, x.shape[-1]), axis=0)

    def upper_cols(wide, narrow, op):
        return jnp.concatenate([wide[:, :half], op(wide[:, half:], narrow)], axis=1)

    keep_wide = (lax.broadcasted_iota(jnp.int32, (half, tq), 0)
                 <= lax.broadcasted_iota(jnp.int32, (half, tq), 1))
    keep_narrow = keep_wide[:, :half]

    def score_pieces(hd, t, c, out):
        q0 = t * tq
        q = q_ref[q0:q0 + tq, head_cols(hd)]
        lane = lax.broadcasted_iota(jnp.int32, q.shape, 1)
        mine = (lane < DIFF_HEAD_DIM) if c == 0 else (lane >= DIFF_HEAD_DIM)
        qc = jnp.where(mine, q, jnp.zeros_like(q))
        full, mpart = [], None
        for k0 in range(0, q0, KEY_BLOCK):
            k1 = min(k0 + KEY_BLOCK, q0)
            s = scores_t(k_ref[k0:k1, head_cols(hd)], qc)
            mblk = fold8(s, jnp.max)
            mpart = mblk if mpart is None else jnp.maximum(mpart, mblk)
            full.append((k0, k1, s))
            yield
        s_diag = jnp.where(keep_wide, scores_t(k_ref[q0:q0 + half, head_cols(hd)], qc), -jnp.inf)
        s_last = jnp.where(keep_narrow,
                           scores_t(k_ref[q0 + half:q0 + tq, head_cols(hd)], qc[half:]), -jnp.inf)
        mblk = fold8(s_diag, jnp.max)
        mpart = mblk if mpart is None else jnp.maximum(mpart, mblk)
        mpart = upper_cols(mpart, fold8(s_last, jnp.max), jnp.maximum)
        out.update(full=full, s_diag=s_diag, s_last=s_last,
                   m=jnp.max(mpart, axis=0, keepdims=True))
        yield

    def value_pieces(hd, t, scores, out):
        q0 = t * tq
        m = scores["m"]
        acc, lpart = None, None
        for k0, k1, s in scores["full"]:
            p = jnp.exp2(s - m)
            lblk = fold8(p, jnp.sum)
            lpart = lblk if lpart is None else lpart + lblk
            part = jnp.dot(vt_ref[hd, :, k0:k1], p.astype(BF16), preferred_element_type=F32)
            acc = part if acc is None else acc + part
            yield
        p_diag = jnp.exp2(scores["s_diag"] - m)
        p_last = jnp.exp2(scores["s_last"] - m[:, half:])
        lblk = fold8(p_diag, jnp.sum)
        lpart = lblk if lpart is None else lpart + lblk
        lpart = upper_cols(lpart, fold8(p_last, jnp.sum), jnp.add)
        part = jnp.dot(vt_ref[hd, :, q0:q0 + half], p_diag.astype(BF16), preferred_element_type=F32)
        acc = part if acc is None else acc + part
        acc = upper_cols(acc, jnp.dot(vt_ref[hd, :, q0 + half:q0 + tq], p_last.astype(BF16),
                                      preferred_element_type=F32), jnp.add)
        out.update(acc=acc, l=jnp.sum(lpart, axis=0, keepdims=True))
        yield

    def finish_tile(hd, t, comp1, comp2):
        o = comp1["acc"] * (1.0 / comp1["l"]) - comp2["acc"] * (lam / comp2["l"])
        o = o * lax.rsqrt(jnp.mean(o * o, axis=0, keepdims=True) + NORM_EPS)
        o_ref[t * tq:(t + 1) * tq, head_cols(hd)] = (o.T * sg_ref[...] * (1.0 - lam0)).astype(BF16)

    def emit_alternating(first, second):
        pending = [first, second]
        while pending:
            for g in list(pending):
                if next(g, StopIteration) is StopIteration:
                    pending.remove(g)

    units = [(hd, t, c) for hd in range(HEADS_PER_STEP) for t in range(seq // tq) for c in range(2)]
    scores = {}
    emit_alternating(score_pieces(*units[0], scores), iter(()))
    done = {}
    for i, (hd, t, c) in enumerate(units):
        nxt = {}
        done[c] = {}
        ahead = score_pieces(*units[i + 1], nxt) if i + 1 < len(units) else iter(())
        emit_alternating(ahead, value_pieces(hd, t, scores, done[c]))
        scores = nxt
        if c == 1:
            finish_tile(hd, t, done[0], done[1])


def _diff_attention(proj, lam_qk, subln_g, later_weights, layer, *, batch, seq, lam0):
    n = batch * seq
    groups = N_DIFF_HEADS // HEADS_PER_STEP
    width = HEADS_PER_STEP * DIFF_V_DIM
    specs = [_slab_specs(w.shape[1], w.shape[2], batch * groups, layer,
                         lambda b, h: b * groups + h) for w in later_weights]
    outs = pl.pallas_call(
        functools.partial(_attn_kernel, lam0=lam0, n_cast=len(later_weights)),
        grid=(batch, groups),
        in_specs=[
            _resident((LAMBDA_VECTORS, DIFF_HEAD_DIM)),
            pl.BlockSpec((seq, width), lambda b, h: (b, h)),
            pl.BlockSpec((seq, width), lambda b, h: (b, groups + h)),
            pl.BlockSpec((seq, width), lambda b, h: (b, 2 * groups + h)),
            _resident((1, DIFF_V_DIM)),
            *(spec_in for spec_in, _ in specs),
        ],
        out_specs=[pl.BlockSpec((seq, width), lambda b, h: (b, h)),
                   *(spec_out for _, spec_out in specs)],
        out_shape=[jax.ShapeDtypeStruct((n, N_DIFF_HEADS * DIFF_V_DIM), BF16),
                   *(jax.ShapeDtypeStruct(w.shape[1:], BF16) for w in later_weights)],
        scratch_shapes=[pltpu.VMEM((HEADS_PER_STEP, DIFF_V_DIM, seq), BF16)],
        compiler_params=pltpu.CompilerParams(
            dimension_semantics=("arbitrary", "arbitrary"), vmem_limit_bytes=VMEM_LIMIT_BYTES),
        name="diff_attention",
    )(lam_qk, proj, proj, proj, subln_g.reshape(1, DIFF_V_DIM), *later_weights)
    return outs[0], outs[1:]


def _mix_ffn_kernel(x_ref, a_ref, u_ref, halo_ref, pw_ref, ps_ref, wo_ref, g2_ref,
                    wg_ref, wu_ref, wd_ref, gf_ref, o_ref, wpool_ref, *, tiles_per_seq, final_norm):
    tm = TOKEN_TILE
    attn_width = a_ref.shape[1]

    @pl.when(pl.program_id(0) == 0)
    def _():
        for g in range(len(POOL_WINDOWS)):
            rows = slice(g * POOL_GROUP_DIM, (g + 1) * POOL_GROUP_DIM)
            scaled = (pw_ref[g].astype(F32) * ps_ref[:, rows]).astype(BF16)
            wpool_ref[rows, :] = jnp.dot(
                scaled, wo_ref[attn_width + g * POOL_GROUP_DIM:attn_width + (g + 1) * POOL_GROUP_DIM, :],
                preferred_element_type=F32).astype(BF16)

    def token_tile(sub, carry):
        _mix_ffn_tile(sub, x_ref, a_ref, u_ref, halo_ref, wpool_ref, wo_ref, g2_ref, wg_ref, wu_ref,
                      wd_ref, gf_ref, o_ref, tiles_per_seq=tiles_per_seq, final_norm=final_norm)
        return carry

    lax.fori_loop(0, x_ref.shape[0] // tm, token_tile, 0)


def _mix_ffn_tile(sub, x_ref, a_ref, u_ref, halo_ref, wpool_ref, wo_ref, g2_ref, wg_ref, wu_ref,
                  wd_ref, gf_ref, o_ref, *, tiles_per_seq, final_norm):
    tm = TOKEN_TILE
    tiles_per_step = x_ref.shape[0] // tm
    attn_width = a_ref.shape[1]
    row0 = pl.multiple_of(sub * tm, tm)
    rows = pl.ds(row0, tm)
    t0 = ((pl.program_id(0) * tiles_per_step + sub) % tiles_per_seq) * tm

    d_model = x_ref.shape[1]
    col_step = d_model // len(POOL_WINDOWS)
    a_parts = []

    u = u_ref[rows, :].astype(F32)
    prev0 = pl.multiple_of(jnp.maximum(row0 - MAX_WINDOW, 0), MAX_WINDOW)
    halo = jnp.where(sub == 0, halo_ref[...], u_ref[pl.ds(prev0, MAX_WINDOW), :]).astype(F32)
    ext = jnp.concatenate([jnp.where(t0 == 0, jnp.zeros_like(halo), halo), u], axis=0)
    pos = t0 + lax.broadcasted_iota(jnp.int32, (tm, 1), 0)
    resids = []
    for g, w in enumerate(POOL_WINDOWS):
        a_parts.append(jnp.dot(a_ref[rows, :], wo_ref[:attn_width, g * col_step:(g + 1) * col_step],
                               preferred_element_type=F32))
        cols = slice(g * POOL_GROUP_DIM, (g + 1) * POOL_GROUP_DIM)
        wsum = ext[:, cols]
        span = 1
        while span < w:
            wsum = wsum + pltpu.roll(wsum, shift=span, axis=0)
            span *= 2
        inv_count = 1.0 / jnp.minimum(pos + 1, w).astype(F32)
        resids.append((wsum[MAX_WINDOW:] * inv_count - u[:, cols]).astype(BF16))
    x1 = (x_ref[rows, :] + jnp.concatenate(a_parts, axis=-1)
          + jnp.dot(jnp.concatenate(resids, axis=-1), wpool_ref[...], preferred_element_type=F32))

    h2 = (x1 * g2_ref[...]).astype(BF16)
    scale = _rms_scale(x1)
    ffn = None
    f0 = 0
    for fc in FF_CHUNKS:
        acts = []
        for c0 in range(f0, f0 + fc, MXU_TILE):
            w_gu = jnp.concatenate([wg_ref[:, c0:c0 + MXU_TILE], wu_ref[:, c0:c0 + MXU_TILE]], axis=1)
            gu = jnp.dot(h2, w_gu, preferred_element_type=F32) * scale
            gate, up = gu[:, :MXU_TILE], gu[:, MXU_TILE:]
            acts.append((gate * (1.0 / (1.0 + jnp.exp(-gate))) * up).astype(BF16))
        act = jnp.concatenate(acts, axis=1)
        part = jnp.dot(act, wd_ref[f0:f0 + fc, :], preferred_element_type=F32)
        ffn = part if ffn is None else ffn + part
        f0 += fc
    x2 = x1 + ffn
    if final_norm:
        x2 = x2 * _rms_scale(x2) * gf_ref[...]
    o_ref[rows, :] = x2


def _mix_ffn(x2d, attn, proj, pool_w, pool_scale, w_out, g2, w_gate, w_up, w_down, final_g,
             *, seq, final_norm):
    n, d = x2d.shape
    d_ff = w_gate.shape[1]
    assert sum(FF_CHUNKS) == d_ff
    assert all(w & (w - 1) == 0 for w in POOL_WINDOWS), "doubling needs power-of-two windows"
    pool_width = len(POOL_WINDOWS) * POOL_GROUP_DIM
    tm = MIX_TILES_PER_STEP * TOKEN_TILE
    u_col = proj.shape[1] // pool_width - 1
    halo_per_tile = tm // MAX_WINDOW
    return pl.pallas_call(
        functools.partial(_mix_ffn_kernel, tiles_per_seq=seq // TOKEN_TILE, final_norm=final_norm),
        grid=(n // tm,),
        in_specs=[
            pl.BlockSpec((tm, d), lambda i: (i, 0)),
            pl.BlockSpec((tm, attn.shape[1]), lambda i: (i, 0)),
            pl.BlockSpec((tm, pool_width), lambda i: (i, u_col)),
            pl.BlockSpec((MAX_WINDOW, pool_width),
                         lambda i: (jnp.maximum(i * halo_per_tile - 1, 0), u_col)),
            _resident(pool_w.shape),
            _resident((1, pool_width)),
            _resident(w_out.shape),
            _resident((1, d)),
            _resident(w_gate.shape),
            _resident(w_up.shape),
            _resident(w_down.shape),
            _resident((1, d)),
        ],
        out_specs=pl.BlockSpec((tm, d), lambda i: (i, 0)),
        out_shape=jax.ShapeDtypeStruct((n, d), F32),
        scratch_shapes=[pltpu.VMEM((pool_width, d), BF16)],
        compiler_params=pltpu.CompilerParams(
            dimension_semantics=("arbitrary",), vmem_limit_bytes=VMEM_LIMIT_BYTES),
        name="mix_ffn",
    )(x2d, attn, proj, proj, pool_w, pool_scale.reshape(1, pool_width), w_out,
      g2.reshape(1, d), w_gate, w_up, w_down, final_g.reshape(1, d))


def kernel(x, norm1_g, w_in, lam_qk, subln_g, pool_w, pool_scale, w_out, norm2_g, w_gate, w_up,
           w_down, final_g):
    batch, seq, d = x.shape
    depth = w_in.shape[0]
    assert seq % (MIX_TILES_PER_STEP * TOKEN_TILE) == 0 and seq % Q_TILE == 0
    assert (batch * seq) % INPROJ_TILE == 0
    x2d = x.reshape(batch * seq, d)
    pool_shape = pool_w.shape[1:]
    later = (pool_w.reshape(depth, -1, pool_shape[-1]), w_out, w_gate, w_up, w_down)
    for layer in range(depth):
        proj = _norm_inproj(x2d, norm1_g[layer], w_in, layer)
        attn, (pw, wo, wg, wu, wd) = _diff_attention(
            proj, lam_qk[layer], subln_g[layer], later, layer, batch=batch, seq=seq,
            lam0=_lambda_init(layer))
        x2d = _mix_ffn(x2d, attn, proj, pw.reshape(pool_shape), pool_scale[layer], wo,
                       norm2_g[layer], wg, wu, wd, final_g, seq=seq,
                       final_norm=(layer == depth - 1))
    return x2d.reshape(batch, seq, d)
```

```python
import functools
import math

import jax
import jax.numpy as jnp
from jax import lax
from jax.experimental import pallas as pl
from jax.experimental.pallas import tpu as pltpu

F32 = jnp.float32
BF16 = jnp.bfloat16

N_DIFF_HEADS = 4
DIFF_HEAD_DIM = 64
DIFF_V_DIM = 2 * DIFF_HEAD_DIM
ATTN_WIDTH = N_DIFF_HEADS * DIFF_V_DIM
Q_SCALE = DIFF_HEAD_DIM ** -0.5 * math.log2(math.e)
POOL_WINDOWS = (2, 4, 8, 16)
POOL_GROUP_DIM = 128
MAX_WINDOW = max(POOL_WINDOWS)
NORM_EPS = 1e-5

F32_SUBLANES = 8
BF16_SUBLANE_TILE = 16
MXU_TILE = 256

VMEM_LIMIT_BYTES = 56 * 1024 * 1024

TOKEN_TILE = 512
MIX_TILES_PER_STEP = 2
INPROJ_TILE = 1024
Q_TILE = 512
KEY_BLOCK = 512
HEADS_PER_STEP = 2
FF_CHUNKS = (1536, 1280)


def _lambda_init(layer):
    return 0.8 - 0.6 * math.exp(-0.3 * layer)


def _rms_scale(x):
    return lax.rsqrt(jnp.mean(x * x, axis=-1, keepdims=True) + NORM_EPS)


def _resident(shape):
    return pl.BlockSpec(shape, lambda *_: (0,) * len(shape), pipeline_mode=pl.Buffered(1))


def _layer_resident(shape, layer):
    return pl.BlockSpec((None, *shape), lambda *_: (layer,) + (0,) * len(shape),
                        pipeline_mode=pl.Buffered(1))


def _norm_inproj_kernel(x_ref, g_ref, w_ref, o_ref, wbf_ref):
    @pl.when(pl.program_id(0) == 0)
    def _():
        wbf_ref[...] = w_ref[...].astype(BF16)

    x = x_ref[...]
    proj = jnp.dot((x * g_ref[...]).astype(BF16), wbf_ref[...], preferred_element_type=F32)
    scale = _rms_scale(x)
    o_ref[:, :ATTN_WIDTH] = (proj[:, :ATTN_WIDTH] * (scale * Q_SCALE)).astype(BF16)
    o_ref[:, ATTN_WIDTH:] = (proj[:, ATTN_WIDTH:] * scale).astype(BF16)


def _norm_inproj(x2d, g, w_in, layer):
    n, d = x2d.shape
    e = w_in.shape[2]
    return pl.pallas_call(
        _norm_inproj_kernel,
        grid=(n // INPROJ_TILE,),
        in_specs=[
            pl.BlockSpec((INPROJ_TILE, d), lambda i: (i, 0)),
            _layer_resident((1, d), layer),
            _layer_resident((d, e), layer),
        ],
        out_specs=pl.BlockSpec((INPROJ_TILE, e), lambda i: (i, 0)),
        out_shape=jax.ShapeDtypeStruct((n, e), BF16),
        scratch_shapes=[pltpu.VMEM((d, e), BF16)],
        compiler_params=pltpu.CompilerParams(
            dimension_semantics=("arbitrary",), vmem_limit_bytes=VMEM_LIMIT_BYTES),
        name="norm_inproj",
    )(x2d, g, w_in)


def _slab_specs(rows, cols, n_steps, layer, step_of):
    hold = next(h for h in range(1, n_steps + 1)
                if n_steps % h == 0 and rows % (n_steps // h) == 0
                and (rows // (n_steps // h)) % BF16_SUBLANE_TILE == 0)
    slab = rows // (n_steps // hold)
    return (pl.BlockSpec((None, slab, cols), lambda *ids: (layer, step_of(*ids) // hold, 0)),
            pl.BlockSpec((slab, cols), lambda *ids: (step_of(*ids) // hold, 0)))


def _attn_kernel(lamqk_ref, q_ref, k_ref, v_ref, sg_ref, *rest, lam0, n_cast):
    cast_in, o_ref, cast_out, vt_ref = rest[:n_cast], rest[n_cast], rest[n_cast + 1:-1], rest[-1]
    for src, dst in zip(cast_in, cast_out):
        dst[...] = src[...].astype(BF16)

    seq = q_ref.shape[0]
    tq = Q_TILE
    half = tq // 2

    def head_cols(hd):
        return slice(hd * DIFF_V_DIM, (hd + 1) * DIFF_V_DIM)

    for hd in range(HEADS_PER_STEP):
        vt_ref[hd] = v_ref[:, head_cols(hd)].astype(F32).T.astype(BF16)

    lq = lamqk_ref[...]
    lam = (jnp.exp(jnp.sum(lq[0:1] * lq[1:2], axis=-1, keepdims=True))
           - jnp.exp(jnp.sum(lq[2:3] * lq[3:4], axis=-1, keepdims=True)) + lam0)

    def scores_t(kb, qc):
        return lax.dot_general(kb, qc, (((1,), (1,)), ((), ())), preferred_element_type=F32)

    def fold8(x, op):
        return op(x.reshape(-1, F32_SUBLANES, x.shape[-1]), axis=0)

    def upper_cols(wide, narrow, op):
        return jnp.concatenate([wide[:, :half], op(wide[:, half:], narrow)], axis=1)

    keep_wide = (lax.broadcasted_iota(jnp.int32, (half, tq), 0)
                 <= lax.broadcasted_iota(jnp.int32, (half, tq), 1))
    keep_narrow = keep_wide[:, :half]

    def score_pieces(hd, t, c, out):
        q0 = t * tq
        q = q_ref[q0:q0 + tq, head_cols(hd)]
        lane = lax.broadcasted_iota(jnp.int32, q.shape, 1)
        mine = (lane < DIFF_HEAD_DIM) if c == 0 else (lane >= DIFF_HEAD_DIM)
        qc = jnp.where(mine, q, jnp.zeros_like(q))
        full, mpart = [], None
        for k0 in range(0, q0, KEY_BLOCK):
            k1 = min(k0 + KEY_BLOCK, q0)
            s = scores_t(k_ref[k0:k1, head_cols(hd)], qc)
            mblk = fold8(s, jnp.max)
            mpart = mblk if mpart is None else jnp.maximum(mpart, mblk)
            full.append((k0, k1, s))
            yield
        s_diag = jnp.where(keep_wide, scores_t(k_ref[q0:q0 + half, head_cols(hd)], qc), -jnp.inf)
        s_last = jnp.where(keep_narrow,
                           scores_t(k_ref[q0 + half:q0 + tq, head_cols(hd)], qc[half:]), -jnp.inf)
        mblk = fold8(s_diag, jnp.max)
        mpart = mblk if mpart is None else jnp.maximum(mpart, mblk)
        mpart = upper_cols(mpart, fold8(s_last, jnp.max), jnp.maximum)
        out.update(full=full, s_diag=s_diag, s_last=s_last,
                   m=jnp.max(mpart, axis=0, keepdims=True))
        yield

    def value_pieces(hd, t, scores, out):
        q0 = t * tq
        m = scores["m"]
        acc, lpart = None, None
        for k0, k1, s in scores["full"]:
            p = jnp.exp2(s - m)
            lblk = fold8(p, jnp.sum)
            lpart = lblk if lpart is None else lpart + lblk
            part = jnp.dot(vt_ref[hd, :, k0:k1], p.astype(BF16), preferred_element_type=F32)
            acc = part if acc is None else acc + part
            yield
        p_diag = jnp.exp2(scores["s_diag"] - m)
        p_last = jnp.exp2(scores["s_last"] - m[:, half:])
        lblk = fold8(p_diag, jnp.sum)
        lpart = lblk if lpart is None else lpart + lblk
        lpart = upper_cols(lpart, fold8(p_last, jnp.sum), jnp.add)
        part = jnp.dot(vt_ref[hd, :, q0:q0 + half], p_diag.astype(BF16), preferred_element_type=F32)
        acc = part if acc is None else acc + part
        acc = upper_cols(acc, jnp.dot(vt_ref[hd, :, q0 + half:q0 + tq], p_last.astype(BF16),
                                      preferred_element_type=F32), jnp.add)
        out.update(acc=acc, l=jnp.sum(lpart, axis=0, keepdims=True))
        yield

    def finish_tile(hd, t, comp1, comp2):
        o = comp1["acc"] * (1.0 / comp1["l"]) - comp2["acc"] * (lam / comp2["l"])
        o = o * lax.rsqrt(jnp.mean(o * o, axis=0, keepdims=True) + NORM_EPS)
        o_ref[t * tq:(t + 1) * tq, head_cols(hd)] = (o.T * sg_ref[...] * (1.0 - lam0)).astype(BF16)

    def emit_alternating(first, second):
        pending = [first, second]
        while pending:
            for g in list(pending):
                if next(g, StopIteration) is StopIteration:
                    pending.remove(g)

    units = [(hd, t, c) for hd in range(HEADS_PER_STEP) for t in range(seq // tq) for c in range(2)]
    scores = {}
    emit_alternating(score_pieces(*units[0], scores), iter(()))
    done = {}
    for i, (hd, t, c) in enumerate(units):
        nxt = {}
        done[c] = {}
        ahead = score_pieces(*units[i + 1], nxt) if i + 1 < len(units) else iter(())
        emit_alternating(ahead, value_pieces(hd, t, scores, done[c]))
        scores = nxt
        if c == 1:
            finish_tile(hd, t, done[0], done[1])


def _diff_attention(proj, lam_qk, subln_g, later_weights, layer, *, batch, seq, lam0):
    n = batch * seq
    groups = N_DIFF_HEADS // HEADS_PER_STEP
    width = HEADS_PER_STEP * DIFF_V_DIM
    specs = [_slab_specs(w.shape[1], w.shape[2], batch * groups, layer,
                         lambda b, h: b * groups + h) for w in later_weights]
    outs = pl.pallas_call(
        functools.partial(_attn_kernel, lam0=lam0, n_cast=len(later_weights)),
        grid=(batch, groups),
        in_specs=[
            _layer_resident(lam_qk.shape[1:], layer),
            pl.BlockSpec((seq, width), lambda b, h: (b, h)),
            pl.BlockSpec((seq, width), lambda b, h: (b, groups + h)),
            pl.BlockSpec((seq, width), lambda b, h: (b, 2 * groups + h)),
            _layer_resident((1, DIFF_V_DIM), layer),
            *(spec_in for spec_in, _ in specs),
        ],
        out_specs=[pl.BlockSpec((seq, width), lambda b, h: (b, h)),
                   *(spec_out for _, spec_out in specs)],
        out_shape=[jax.ShapeDtypeStruct((n, N_DIFF_HEADS * DIFF_V_DIM), BF16),
                   *(jax.ShapeDtypeStruct(w.shape[1:], BF16) for w in later_weights)],
        scratch_shapes=[pltpu.VMEM((HEADS_PER_STEP, DIFF_V_DIM, seq), BF16)],
        compiler_params=pltpu.CompilerParams(
            dimension_semantics=("arbitrary", "arbitrary"), vmem_limit_bytes=VMEM_LIMIT_BYTES),
        name="diff_attention",
    )(lam_qk, proj, proj, proj, subln_g, *later_weights)
    return outs[0], outs[1:]


def _mix_ffn_kernel(x_ref, a_ref, u_ref, halo_ref, pw_ref, ps_ref, wo_ref, g2_ref,
                    wg_ref, wu_ref, wd_ref, gf_ref, o_ref, wpool_ref, *, tiles_per_seq, final_norm):
    tm = TOKEN_TILE
    attn_width = a_ref.shape[1]

    @pl.when(pl.program_id(0) == 0)
    def _():
        for g in range(len(POOL_WINDOWS)):
            rows = slice(g * POOL_GROUP_DIM, (g + 1) * POOL_GROUP_DIM)
            scaled = (pw_ref[rows, :].astype(F32) * ps_ref[:, rows]).astype(BF16)
            wpool_ref[rows, :] = jnp.dot(
                scaled, wo_ref[attn_width + g * POOL_GROUP_DIM:attn_width + (g + 1) * POOL_GROUP_DIM, :],
                preferred_element_type=F32).astype(BF16)

    def token_tile(sub, carry):
        _mix_ffn_tile(sub, x_ref, a_ref, u_ref, halo_ref, wpool_ref, wo_ref, g2_ref, wg_ref, wu_ref,
                      wd_ref, gf_ref, o_ref, tiles_per_seq=tiles_per_seq, final_norm=final_norm)
        return carry

    lax.fori_loop(0, x_ref.shape[0] // tm, token_tile, 0)


def _mix_ffn_tile(sub, x_ref, a_ref, u_ref, halo_ref, wpool_ref, wo_ref, g2_ref, wg_ref, wu_ref,
                  wd_ref, gf_ref, o_ref, *, tiles_per_seq, final_norm):
    tm = TOKEN_TILE
    tiles_per_step = x_ref.shape[0] // tm
    attn_width = a_ref.shape[1]
    row0 = pl.multiple_of(sub * tm, tm)
    rows = pl.ds(row0, tm)
    t0 = ((pl.program_id(0) * tiles_per_step + sub) % tiles_per_seq) * tm

    d_model = x_ref.shape[1]
    col_step = d_model // len(POOL_WINDOWS)
    a_parts = []

    u = u_ref[rows, :].astype(F32)
    prev0 = pl.multiple_of(jnp.maximum(row0 - MAX_WINDOW, 0), MAX_WINDOW)
    halo = jnp.where(sub == 0, halo_ref[...], u_ref[pl.ds(prev0, MAX_WINDOW), :]).astype(F32)
    ext = jnp.concatenate([jnp.where(t0 == 0, jnp.zeros_like(halo), halo), u], axis=0)
    pos = t0 + lax.broadcasted_iota(jnp.int32, (tm, 1), 0)
    resids = []
    for g, w in enumerate(POOL_WINDOWS):
        a_parts.append(jnp.dot(a_ref[rows, :], wo_ref[:attn_width, g * col_step:(g + 1) * col_step],
                               preferred_element_type=F32))
        cols = slice(g * POOL_GROUP_DIM, (g + 1) * POOL_GROUP_DIM)
        wsum = ext[:, cols]
        span = 1
        while span < w:
            wsum = wsum + pltpu.roll(wsum, shift=span, axis=0)
            span *= 2
        inv_count = 1.0 / jnp.minimum(pos + 1, w).astype(F32)
        resids.append((wsum[MAX_WINDOW:] * inv_count - u[:, cols]).astype(BF16))
    x1 = (x_ref[rows, :] + jnp.concatenate(a_parts, axis=-1)
          + jnp.dot(jnp.concatenate(resids, axis=-1), wpool_ref[...], preferred_element_type=F32))

    h2 = (x1 * g2_ref[...]).astype(BF16)
    scale = _rms_scale(x1)
    ffn = None
    f0 = 0
    for fc in FF_CHUNKS:
        acts = []
        for c0 in range(f0, f0 + fc, MXU_TILE):
            w_gu = jnp.concatenate([wg_ref[:, c0:c0 + MXU_TILE], wu_ref[:, c0:c0 + MXU_TILE]], axis=1)
            gu = jnp.dot(h2, w_gu, preferred_element_type=F32) * scale
            gate, up = gu[:, :MXU_TILE], gu[:, MXU_TILE:]
            acts.append((gate * (1.0 / (1.0 + jnp.exp(-gate))) * up).astype(BF16))
        act = jnp.concatenate(acts, axis=1)
        part = jnp.dot(act, wd_ref[f0:f0 + fc, :], preferred_element_type=F32)
        ffn = part if ffn is None else ffn + part
        f0 += fc
    x2 = x1 + ffn
    if final_norm:
        x2 = x2 * _rms_scale(x2) * gf_ref[...]
    o_ref[rows, :] = x2


def _mix_ffn(x2d, attn, proj, pool_w, pool_scale, w_out, g2, w_gate, w_up, w_down, final_g,
             layer, *, seq, final_norm):
    n, d = x2d.shape
    d_ff = w_gate.shape[1]
    assert sum(FF_CHUNKS) == d_ff
    assert all(w & (w - 1) == 0 for w in POOL_WINDOWS), "doubling needs power-of-two windows"
    pool_width = len(POOL_WINDOWS) * POOL_GROUP_DIM
    tm = MIX_TILES_PER_STEP * TOKEN_TILE
    u_col = proj.shape[1] // pool_width - 1
    halo_per_tile = tm // MAX_WINDOW
    return pl.pallas_call(
        functools.partial(_mix_ffn_kernel, tiles_per_seq=seq // TOKEN_TILE, final_norm=final_norm),
        grid=(n // tm,),
        in_specs=[
            pl.BlockSpec((tm, d), lambda i: (i, 0)),
            pl.BlockSpec((tm, attn.shape[1]), lambda i: (i, 0)),
            pl.BlockSpec((tm, pool_width), lambda i: (i, u_col)),
            pl.BlockSpec((MAX_WINDOW, pool_width),
                         lambda i: (jnp.maximum(i * halo_per_tile - 1, 0), u_col)),
            _resident(pool_w.shape),
            _layer_resident((1, pool_width), layer),
            _resident(w_out.shape),
            _layer_resident((1, d), layer),
            _resident(w_gate.shape),
            _resident(w_up.shape),
            _resident(w_down.shape),
            _resident((1, d)),
        ],
        out_specs=pl.BlockSpec((tm, d), lambda i: (i, 0)),
        out_shape=jax.ShapeDtypeStruct((n, d), F32),
        scratch_shapes=[pltpu.VMEM((pool_width, d), BF16)],
        compiler_params=pltpu.CompilerParams(
            dimension_semantics=("arbitrary",), vmem_limit_bytes=VMEM_LIMIT_BYTES),
        name="mix_ffn",
    )(x2d, attn, proj, proj, pool_w, pool_scale, w_out, g2, w_gate, w_up, w_down, final_g)


def kernel(x, norm1_g, w_in, lam_qk, subln_g, pool_w, pool_scale, w_out, norm2_g, w_gate, w_up,
           w_down, final_g):
    batch, seq, d = x.shape
    depth = w_in.shape[0]
    assert seq % (MIX_TILES_PER_STEP * TOKEN_TILE) == 0 and seq % Q_TILE == 0
    assert (batch * seq) % INPROJ_TILE == 0
    x2d = x.reshape(batch * seq, d)
    later = (pool_w.reshape(depth, -1, pool_w.shape[-1]), w_out, w_gate, w_up, w_down)
    norm1_g, subln_g, pool_scale, norm2_g = (
        p.reshape(depth, 1, -1) for p in (norm1_g, subln_g, pool_scale, norm2_g))
    final_g = final_g.reshape(1, d)
    for layer in range(depth):
        proj = _norm_inproj(x2d, norm1_g, w_in, layer)
        attn, (pw, wo, wg, wu, wd) = _diff_attention(
            proj, lam_qk, subln_g, later, layer, batch=batch, seq=seq, lam0=_lambda_init(layer))
        x2d = _mix_ffn(x2d, attn, proj, pw, pool_scale, wo, norm2_g, wg, wu, wd, final_g, layer,
                       seq=seq, final_norm=(layer == depth - 1))
    return x2d.reshape(batch, seq, d)
```

```python
import functools
import math

import jax
import jax.numpy as jnp
from jax import lax
from jax.experimental import pallas as pl
from jax.experimental.pallas import tpu as pltpu

F32 = jnp.float32
BF16 = jnp.bfloat16

N_DIFF_HEADS = 4
DIFF_HEAD_DIM = 64
DIFF_V_DIM = 2 * DIFF_HEAD_DIM
ATTN_WIDTH = N_DIFF_HEADS * DIFF_V_DIM
Q_SCALE = DIFF_HEAD_DIM ** -0.5 * math.log2(math.e)
POOL_WINDOWS = (2, 4, 8, 16)
POOL_GROUP_DIM = 128
MAX_WINDOW = max(POOL_WINDOWS)
NORM_EPS = 1e-5

F32_SUBLANES = 8
BF16_SUBLANE_TILE = 16
MXU_TILE = 256

VMEM_LIMIT_BYTES = 56 * 1024 * 1024

TOKEN_TILE = 512
MIX_TILES_PER_STEP = 2
INPROJ_TILE = 1024
Q_TILE = 512
KEY_BLOCK = 512
HEADS_PER_STEP = 2
FF_CHUNKS = (1536, 1280)


def _lambda_init(layer):
    return 0.8 - 0.6 * math.exp(-0.3 * layer)


def _rms_scale(x):
    return lax.rsqrt(jnp.mean(x * x, axis=-1, keepdims=True) + NORM_EPS)


def _resident(shape):
    return pl.BlockSpec(shape, lambda *_: (0,) * len(shape), pipeline_mode=pl.Buffered(1))


def _layer_resident(shape, layer):
    return pl.BlockSpec((None, *shape), lambda *_: (layer,) + (0,) * len(shape),
                        pipeline_mode=pl.Buffered(1))


def _norm_inproj_kernel(x_ref, g_ref, w_ref, o_ref, wbf_ref, *, layer):
    @pl.when(pl.program_id(0) == 0)
    def _():
        wbf_ref[...] = w_ref[...].astype(BF16)

    x = x_ref[...]
    proj = jnp.dot((x * g_ref[layer:layer + 1, :]).astype(BF16), wbf_ref[...],
                   preferred_element_type=F32)
    scale = _rms_scale(x)
    o_ref[:, :ATTN_WIDTH] = (proj[:, :ATTN_WIDTH] * (scale * Q_SCALE)).astype(BF16)
    o_ref[:, ATTN_WIDTH:] = (proj[:, ATTN_WIDTH:] * scale).astype(BF16)


def _norm_inproj(x2d, g, w_in, layer):
    n, d = x2d.shape
    e = w_in.shape[2]
    return pl.pallas_call(
        functools.partial(_norm_inproj_kernel, layer=layer),
        grid=(n // INPROJ_TILE,),
        in_specs=[
            pl.BlockSpec((INPROJ_TILE, d), lambda i: (i, 0)),
            _resident(g.shape),
            _layer_resident((d, e), layer),
        ],
        out_specs=pl.BlockSpec((INPROJ_TILE, e), lambda i: (i, 0)),
        out_shape=jax.ShapeDtypeStruct((n, e), BF16),
        scratch_shapes=[pltpu.VMEM((d, e), BF16)],
        compiler_params=pltpu.CompilerParams(
            dimension_semantics=("arbitrary",), vmem_limit_bytes=VMEM_LIMIT_BYTES),
        name="norm_inproj",
    )(x2d, g, w_in)


def _slab_specs(rows, cols, n_steps, layer, step_of):
    hold = next(h for h in range(1, n_steps + 1)
                if n_steps % h == 0 and rows % (n_steps // h) == 0
                and (rows // (n_steps // h)) % BF16_SUBLANE_TILE == 0)
    slab = rows // (n_steps // hold)
    return (pl.BlockSpec((None, slab, cols), lambda *ids: (layer, step_of(*ids) // hold, 0)),
            pl.BlockSpec((slab, cols), lambda *ids: (step_of(*ids) // hold, 0)))


def _attn_kernel(lamqk_ref, q_ref, k_ref, v_ref, sg_ref, *rest, lam0, layer, n_cast):
    cast_in, o_ref, cast_out, vt_ref = rest[:n_cast], rest[n_cast], rest[n_cast + 1:-1], rest[-1]
    for src, dst in zip(cast_in, cast_out):
        dst[...] = src[...].astype(BF16)

    seq = q_ref.shape[0]
    tq = Q_TILE
    half = tq // 2

    def head_cols(hd):
        return slice(hd * DIFF_V_DIM, (hd + 1) * DIFF_V_DIM)

    for hd in range(HEADS_PER_STEP):
        vt_ref[hd] = v_ref[:, head_cols(hd)].astype(F32).T.astype(BF16)

    sub_ln_gain = sg_ref[layer:layer + 1, :]
    lq = lamqk_ref[...]
    lam = (jnp.exp(jnp.sum(lq[0:1] * lq[1:2], axis=-1, keepdims=True))
           - jnp.exp(jnp.sum(lq[2:3] * lq[3:4], axis=-1, keepdims=True)) + lam0)

    def scores_t(kb, qc):
        return lax.dot_general(kb, qc, (((1,), (1,)), ((), ())), preferred_element_type=F32)

    def fold8(x, op):
        return op(x.reshape(-1, F32_SUBLANES, x.shape[-1]), axis=0)

    def upper_cols(wide, narrow, op):
        return jnp.concatenate([wide[:, :half], op(wide[:, half:], narrow)], axis=1)

    keep_wide = (lax.broadcasted_iota(jnp.int32, (half, tq), 0)
                 <= lax.broadcasted_iota(jnp.int32, (half, tq), 1))
    keep_narrow = keep_wide[:, :half]

    def score_pieces(hd, t, c, out):
        q0 = t * tq
        q = q_ref[q0:q0 + tq, head_cols(hd)]
        lane = lax.broadcasted_iota(jnp.int32, q.shape, 1)
        mine = (lane < DIFF_HEAD_DIM) if c == 0 else (lane >= DIFF_HEAD_DIM)
        qc = jnp.where(mine, q, jnp.zeros_like(q))
        full, mpart = [], None
        for k0 in range(0, q0, KEY_BLOCK):
            k1 = min(k0 + KEY_BLOCK, q0)
            s = scores_t(k_ref[k0:k1, head_cols(hd)], qc)
            mblk = fold8(s, jnp.max)
            mpart = mblk if mpart is None else jnp.maximum(mpart, mblk)
            full.append((k0, k1, s))
            yield
        s_diag = jnp.where(keep_wide, scores_t(k_ref[q0:q0 + half, head_cols(hd)], qc), -jnp.inf)
        s_last = jnp.where(keep_narrow,
                           scores_t(k_ref[q0 + half:q0 + tq, head_cols(hd)], qc[half:]), -jnp.inf)
        mblk = fold8(s_diag, jnp.max)
        mpart = mblk if mpart is None else jnp.maximum(mpart, mblk)
        mpart = upper_cols(mpart, fold8(s_last, jnp.max), jnp.maximum)
        out.update(full=full, s_diag=s_diag, s_last=s_last,
                   m=jnp.max(mpart, axis=0, keepdims=True))
        yield

    def value_pieces(hd, t, scores, out):
        q0 = t * tq
        m = scores["m"]
        acc, lpart = None, None
        for k0, k1, s in scores["full"]:
            p = jnp.exp2(s - m)
            lblk = fold8(p, jnp.sum)
            lpart = lblk if lpart is None else lpart + lblk
            part = jnp.dot(vt_ref[hd, :, k0:k1], p.astype(BF16), preferred_element_type=F32)
            acc = part if acc is None else acc + part
            yield
        p_diag = jnp.exp2(scores["s_diag"] - m)
        p_last = jnp.exp2(scores["s_last"] - m[:, half:])
        lblk = fold8(p_diag, jnp.sum)
        lpart = lblk if lpart is None else lpart + lblk
        lpart = upper_cols(lpart, fold8(p_last, jnp.sum), jnp.add)
        part = jnp.dot(vt_ref[hd, :, q0:q0 + half], p_diag.astype(BF16), preferred_element_type=F32)
        acc = part if acc is None else acc + part
        acc = upper_cols(acc, jnp.dot(vt_ref[hd, :, q0 + half:q0 + tq], p_last.astype(BF16),
                                      preferred_element_type=F32), jnp.add)
        out.update(acc=acc, l=jnp.sum(lpart, axis=0, keepdims=True))
        yield

    def finish_tile(hd, t, comp1, comp2):
        o = comp1["acc"] * (1.0 / comp1["l"]) - comp2["acc"] * (lam / comp2["l"])
        o = o * lax.rsqrt(jnp.mean(o * o, axis=0, keepdims=True) + NORM_EPS)
        o_ref[t * tq:(t + 1) * tq, head_cols(hd)] = (o.T * sub_ln_gain * (1.0 - lam0)).astype(BF16)

    def emit_alternating(first, second):
        pending = [first, second]
        while pending:
            for g in list(pending):
                if next(g, StopIteration) is StopIteration:
                    pending.remove(g)

    units = [(hd, t, c) for hd in range(HEADS_PER_STEP) for t in range(seq // tq) for c in range(2)]
    scores = {}
    emit_alternating(score_pieces(*units[0], scores), iter(()))
    done = {}
    for i, (hd, t, c) in enumerate(units):
        nxt = {}
        done[c] = {}
        ahead = score_pieces(*units[i + 1], nxt) if i + 1 < len(units) else iter(())
        emit_alternating(ahead, value_pieces(hd, t, scores, done[c]))
        scores = nxt
        if c == 1:
            finish_tile(hd, t, done[0], done[1])


def _diff_attention(proj, lam_qk, subln_g, later_weights, layer, *, batch, seq, lam0):
    n = batch * seq
    groups = N_DIFF_HEADS // HEADS_PER_STEP
    width = HEADS_PER_STEP * DIFF_V_DIM
    specs = [_slab_specs(w.shape[1], w.shape[2], batch * groups, layer,
                         lambda b, h: b * groups + h) for w in later_weights]
    outs = pl.pallas_call(
        functools.partial(_attn_kernel, lam0=lam0, layer=layer, n_cast=len(later_weights)),
        grid=(batch, groups),
        in_specs=[
            _layer_resident(lam_qk.shape[1:], layer),
            pl.BlockSpec((seq, width), lambda b, h: (b, h)),
            pl.BlockSpec((seq, width), lambda b, h: (b, groups + h)),
            pl.BlockSpec((seq, width), lambda b, h: (b, 2 * groups + h)),
            _resident(subln_g.shape),
            *(spec_in for spec_in, _ in specs),
        ],
        out_specs=[pl.BlockSpec((seq, width), lambda b, h: (b, h)),
                   *(spec_out for _, spec_out in specs)],
        out_shape=[jax.ShapeDtypeStruct((n, N_DIFF_HEADS * DIFF_V_DIM), BF16),
                   *(jax.ShapeDtypeStruct(w.shape[1:], BF16) for w in later_weights)],
        scratch_shapes=[pltpu.VMEM((HEADS_PER_STEP, DIFF_V_DIM, seq), BF16)],
        compiler_params=pltpu.CompilerParams(
            dimension_semantics=("arbitrary", "arbitrary"), vmem_limit_bytes=VMEM_LIMIT_BYTES),
        name="diff_attention",
    )(lam_qk, proj, proj, proj, subln_g, *later_weights)
    return outs[0], outs[1:]


def _mix_ffn_kernel(x_ref, a_ref, u_ref, halo_ref, pw_ref, ps_ref, wo_ref, g2_ref,
                    wg_ref, wu_ref, wd_ref, gf_ref, o_ref, wpool_ref,
                    *, layer, tiles_per_seq, final_norm):
    tm = TOKEN_TILE
    attn_width = a_ref.shape[1]

    @pl.when(pl.program_id(0) == 0)
    def _():
        for g in range(len(POOL_WINDOWS)):
            rows = slice(g * POOL_GROUP_DIM, (g + 1) * POOL_GROUP_DIM)
            scaled = (pw_ref[rows, :].astype(F32) * ps_ref[layer:layer + 1, rows]).astype(BF16)
            wpool_ref[rows, :] = jnp.dot(
                scaled, wo_ref[attn_width + g * POOL_GROUP_DIM:attn_width + (g + 1) * POOL_GROUP_DIM, :],
                preferred_element_type=F32).astype(BF16)

    def token_tile(sub, carry):
        _mix_ffn_tile(sub, x_ref, a_ref, u_ref, halo_ref, wpool_ref, wo_ref, g2_ref, wg_ref, wu_ref,
                      wd_ref, gf_ref, o_ref, layer=layer, tiles_per_seq=tiles_per_seq,
                      final_norm=final_norm)
        return carry

    lax.fori_loop(0, x_ref.shape[0] // tm, token_tile, 0)


def _mix_ffn_tile(sub, x_ref, a_ref, u_ref, halo_ref, wpool_ref, wo_ref, g2_ref, wg_ref, wu_ref,
                  wd_ref, gf_ref, o_ref, *, layer, tiles_per_seq, final_norm):
    tm = TOKEN_TILE
    tiles_per_step = x_ref.shape[0] // tm
    attn_width = a_ref.shape[1]
    row0 = pl.multiple_of(sub * tm, tm)
    rows = pl.ds(row0, tm)
    t0 = ((pl.program_id(0) * tiles_per_step + sub) % tiles_per_seq) * tm

    d_model = x_ref.shape[1]
    col_step = d_model // len(POOL_WINDOWS)
    a_parts = []

    u = u_ref[rows, :].astype(F32)
    prev0 = pl.multiple_of(jnp.maximum(row0 - MAX_WINDOW, 0), MAX_WINDOW)
    halo = jnp.where(sub == 0, halo_ref[...], u_ref[pl.ds(prev0, MAX_WINDOW), :]).astype(F32)
    ext = jnp.concatenate([jnp.where(t0 == 0, jnp.zeros_like(halo), halo), u], axis=0)
    pos = t0 + lax.broadcasted_iota(jnp.int32, (tm, 1), 0)
    resids = []
    for g, w in enumerate(POOL_WINDOWS):
        a_parts.append(jnp.dot(a_ref[rows, :], wo_ref[:attn_width, g * col_step:(g + 1) * col_step],
                               preferred_element_type=F32))
        cols = slice(g * POOL_GROUP_DIM, (g + 1) * POOL_GROUP_DIM)
        wsum = ext[:, cols]
        span = 1
        while span < w:
            wsum = wsum + pltpu.roll(wsum, shift=span, axis=0)
            span *= 2
        inv_count = 1.0 / jnp.minimum(pos + 1, w).astype(F32)
        resids.append((wsum[MAX_WINDOW:] * inv_count - u[:, cols]).astype(BF16))
    x1 = (x_ref[rows, :] + jnp.concatenate(a_parts, axis=-1)
          + jnp.dot(jnp.concatenate(resids, axis=-1), wpool_ref[...], preferred_element_type=F32))

    h2 = (x1 * g2_ref[layer:layer + 1, :]).astype(BF16)
    scale = _rms_scale(x1)
    ffn = None
    f0 = 0
    for fc in FF_CHUNKS:
        acts = []
        for c0 in range(f0, f0 + fc, MXU_TILE):
            w_gu = jnp.concatenate([wg_ref[:, c0:c0 + MXU_TILE], wu_ref[:, c0:c0 + MXU_TILE]], axis=1)
            gu = jnp.dot(h2, w_gu, preferred_element_type=F32) * scale
            gate, up = gu[:, :MXU_TILE], gu[:, MXU_TILE:]
            acts.append((gate * (1.0 / (1.0 + jnp.exp(-gate))) * up).astype(BF16))
        act = jnp.concatenate(acts, axis=1)
        part = jnp.dot(act, wd_ref[f0:f0 + fc, :], preferred_element_type=F32)
        ffn = part if ffn is None else ffn + part
        f0 += fc
    x2 = x1 + ffn
    if final_norm:
        x2 = x2 * _rms_scale(x2) * gf_ref[...]
    o_ref[rows, :] = x2


def _mix_ffn(x2d, attn, proj, pool_w, pool_scale, w_out, g2, w_gate, w_up, w_down, final_g,
             layer, *, seq, final_norm):
    n, d = x2d.shape
    d_ff = w_gate.shape[1]
    assert sum(FF_CHUNKS) == d_ff
    assert all(w & (w - 1) == 0 for w in POOL_WINDOWS), "doubling needs power-of-two windows"
    pool_width = len(POOL_WINDOWS) * POOL_GROUP_DIM
    tm = MIX_TILES_PER_STEP * TOKEN_TILE
    u_col = proj.shape[1] // pool_width - 1
    halo_per_tile = tm // MAX_WINDOW
    return pl.pallas_call(
        functools.partial(_mix_ffn_kernel, layer=layer, tiles_per_seq=seq // TOKEN_TILE,
                          final_norm=final_norm),
        grid=(n // tm,),
        in_specs=[
            pl.BlockSpec((tm, d), lambda i: (i, 0)),
            pl.BlockSpec((tm, attn.shape[1]), lambda i: (i, 0)),
            pl.BlockSpec((tm, pool_width), lambda i: (i, u_col)),
            pl.BlockSpec((MAX_WINDOW, pool_width),
                         lambda i: (jnp.maximum(i * halo_per_tile - 1, 0), u_col)),
            _resident(pool_w.shape),
            _resident(pool_scale.shape),
            _resident(w_out.shape),
            _resident(g2.shape),
            _resident(w_gate.shape),
            _resident(w_up.shape),
            _resident(w_down.shape),
            _resident((1, d)),
        ],
        out_specs=pl.BlockSpec((tm, d), lambda i: (i, 0)),
        out_shape=jax.ShapeDtypeStruct((n, d), F32),
        scratch_shapes=[pltpu.VMEM((pool_width, d), BF16)],
        compiler_params=pltpu.CompilerParams(
            dimension_semantics=("arbitrary",), vmem_limit_bytes=VMEM_LIMIT_BYTES),
        name="mix_ffn",
    )(x2d, attn, proj, proj, pool_w, pool_scale, w_out, g2, w_gate, w_up, w_down, final_g)


def kernel(x, norm1_g, w_in, lam_qk, subln_g, pool_w, pool_scale, w_out, norm2_g, w_gate, w_up,
           w_down, final_g):
    batch, seq, d = x.shape
    depth = w_in.shape[0]
    assert seq % (MIX_TILES_PER_STEP * TOKEN_TILE) == 0 and seq % Q_TILE == 0
    assert (batch * seq) % INPROJ_TILE == 0
    x2d = x.reshape(batch * seq, d)
    later = (pool_w.reshape(depth, -1, pool_w.shape[-1]), w_out, w_gate, w_up, w_down)
    final_g = final_g.reshape(1, d)
    for layer in range(depth):
        proj = _norm_inproj(x2d, norm1_g, w_in, layer)
        attn, (pw, wo, wg, wu, wd) = _diff_attention(
            proj, lam_qk, subln_g, later, layer, batch=batch, seq=seq, lam0=_lambda_init(layer))
        x2d = _mix_ffn(x2d, attn, proj, pw, pool_scale, wo, norm2_g, wg, wu, wd, final_g, layer,
                       seq=seq, final_norm=(layer == depth - 1))
    return x2d.reshape(batch, seq, d)
```

```python
import functools
import math

import jax
import jax.numpy as jnp
from jax import lax
from jax.experimental import pallas as pl
from jax.experimental.pallas import tpu as pltpu

F32 = jnp.float32
BF16 = jnp.bfloat16

N_DIFF_HEADS = 4
DIFF_HEAD_DIM = 64
DIFF_V_DIM = 2 * DIFF_HEAD_DIM
ATTN_WIDTH = N_DIFF_HEADS * DIFF_V_DIM
Q_SCALE = DIFF_HEAD_DIM ** -0.5 * math.log2(math.e)
POOL_WINDOWS = (2, 4, 8, 16)
POOL_GROUP_DIM = 128
MAX_WINDOW = max(POOL_WINDOWS)
NORM_EPS = 1e-5

F32_SUBLANES = 8
BF16_SUBLANE_TILE = 16
MXU_TILE = 256

VMEM_LIMIT_BYTES = 60 * 1024 * 1024

TOKEN_TILE = 512
MIX_TILES_PER_STEP = 2
Q_TILE = 512
KEY_BLOCK = 512
HEADS_PER_STEP = 2
FF_CHUNKS = (1536, 1280)


def _lambda_init(layer):
    return 0.8 - 0.6 * math.exp(-0.3 * layer)


def _rms_scale(x):
    return lax.rsqrt(jnp.mean(x * x, axis=-1, keepdims=True) + NORM_EPS)


def _resident(shape):
    return pl.BlockSpec(shape, lambda *_: (0,) * len(shape), pipeline_mode=pl.Buffered(1))


def _layer_resident(shape, layer):
    return pl.BlockSpec((None, *shape), lambda *_: (layer,) + (0,) * len(shape),
                        pipeline_mode=pl.Buffered(1))


def _slab_specs(rows, cols, n_steps, layer, step_of):
    hold = next(h for h in range(1, n_steps + 1)
                if n_steps % h == 0 and rows % (n_steps // h) == 0
                and (rows // (n_steps // h)) % BF16_SUBLANE_TILE == 0)
    slab = rows // (n_steps // hold)
    return (pl.BlockSpec((None, slab, cols), lambda *ids: (layer, step_of(*ids) // hold, 0)),
            pl.BlockSpec((slab, cols), lambda *ids: (step_of(*ids) // hold, 0)))


def _proj_attn_kernel(x_ref, g_ref, wq_ref, wk_ref, wv_ref, wu_ref, lamqk_ref, sg_ref, *rest,
                      lam0, layer, n_cast):
    cast_in = rest[:n_cast]
    o_ref, u_ref = rest[n_cast], rest[n_cast + 1]
    cast_out = rest[n_cast + 2:2 * n_cast + 2]
    q_ref, k_ref, vt_ref = rest[2 * n_cast + 2:]
    for src, dst in zip(cast_in, cast_out):
        dst[...] = src[...].astype(BF16)

    seq = x_ref.shape[0]
    tq = Q_TILE
    half = tq // 2

    sections = [w[...].astype(BF16) for w in (wq_ref, wk_ref, wv_ref, wu_ref)]

    def head_cols(hd):
        return slice(hd * DIFF_V_DIM, (hd + 1) * DIFF_V_DIM)

    normed = {}

    def normalised(t):
        if t not in normed:
            x = x_ref[t * tq:(t + 1) * tq, :]
            normed[t] = ((x * g_ref[layer:layer + 1, :]).astype(BF16), _rms_scale(x))
        return normed[t]

    def project(t, section):
        rows = slice(t * tq, (t + 1) * tq)
        h, scale = normalised(t)
        out = jnp.dot(h, sections[section], preferred_element_type=F32) * scale
        if section == 0:
            q_ref[rows, :] = (out * Q_SCALE).astype(BF16)
        elif section == 1:
            k_ref[rows, :] = out.astype(BF16)
        elif section == 2:
            for hd in range(HEADS_PER_STEP):
                vt_ref[hd, :, rows] = out[:, head_cols(hd)].T.astype(BF16)
        else:
            u_ref[rows, :] = out.astype(BF16)
        yield

    sub_ln_gain = sg_ref[layer:layer + 1, :]
    lq = lamqk_ref[...]
    lam = (jnp.exp(jnp.sum(lq[0:1] * lq[1:2], axis=-1, keepdims=True))
           - jnp.exp(jnp.sum(lq[2:3] * lq[3:4], axis=-1, keepdims=True)) + lam0)

    def scores_t(kb, qc):
        return lax.dot_general(kb, qc, (((1,), (1,)), ((), ())), preferred_element_type=F32)

    def fold8(x, op):
        return op(x.reshape(-1, F32_SUBLANES, x.shape[-1]), axis=0)

    def upper_cols(wide, narrow, op):
        return jnp.concatenate([wide[:, :half], op(wide[:, half:], narrow)], axis=1)

    keep_wide = (lax.broadcasted_iota(jnp.int32, (half, tq), 0)
                 <= lax.broadcasted_iota(jnp.int32, (half, tq), 1))
    keep_narrow = keep_wide[:, :half]

    def score_pieces(hd, t, c, out):
        q0 = t * tq
        q = q_ref[q0:q0 + tq, head_cols(hd)]
        lane = lax.broadcasted_iota(jnp.int32, q.shape, 1)
        mine = (lane < DIFF_HEAD_DIM) if c == 0 else (lane >= DIFF_HEAD_DIM)
        qc = jnp.where(mine, q, jnp.zeros_like(q))
        full, mpart = [], None
        for k0 in range(0, q0, KEY_BLOCK):
            k1 = min(k0 + KEY_BLOCK, q0)
            s = scores_t(k_ref[k0:k1, head_cols(hd)], qc)
            mblk = fold8(s, jnp.max)
            mpart = mblk if mpart is None else jnp.maximum(mpart, mblk)
            full.append((k0, k1, s))
            yield
        s_diag = jnp.where(keep_wide, scores_t(k_ref[q0:q0 + half, head_cols(hd)], qc), -jnp.inf)
        s_last = jnp.where(keep_narrow,
                           scores_t(k_ref[q0 + half:q0 + tq, head_cols(hd)], qc[half:]), -jnp.inf)
        mblk = fold8(s_diag, jnp.max)
        mpart = mblk if mpart is None else jnp.maximum(mpart, mblk)
        mpart = upper_cols(mpart, fold8(s_last, jnp.max), jnp.maximum)
        out.update(full=full, s_diag=s_diag, s_last=s_last,
                   m=jnp.max(mpart, axis=0, keepdims=True))
        yield

    def value_pieces(hd, t, scores, out):
        q0 = t * tq
        m = scores["m"]
        acc, lpart = None, None
        for k0, k1, s in scores["full"]:
            p = jnp.exp2(s - m)
            lblk = fold8(p, jnp.sum)
            lpart = lblk if lpart is None else lpart + lblk
            part = jnp.dot(vt_ref[hd, :, k0:k1], p.astype(BF16), preferred_element_type=F32)
            acc = part if acc is None else acc + part
            yield
        p_diag = jnp.exp2(scores["s_diag"] - m)
        p_last = jnp.exp2(scores["s_last"] - m[:, half:])
        lblk = fold8(p_diag, jnp.sum)
        lpart = lblk if lpart is None else lpart + lblk
        lpart = upper_cols(lpart, fold8(p_last, jnp.sum), jnp.add)
        part = jnp.dot(vt_ref[hd, :, q0:q0 + half], p_diag.astype(BF16), preferred_element_type=F32)
        acc = part if acc is None else acc + part
        acc = upper_cols(acc, jnp.dot(vt_ref[hd, :, q0 + half:q0 + tq], p_last.astype(BF16),
                                      preferred_element_type=F32), jnp.add)
        out.update(acc=acc, l=jnp.sum(lpart, axis=0, keepdims=True))
        yield

    def finish_tile(hd, t, comp1, comp2):
        o = comp1["acc"] * (1.0 / comp1["l"]) - comp2["acc"] * (lam / comp2["l"])
        o = o * lax.rsqrt(jnp.mean(o * o, axis=0, keepdims=True) + NORM_EPS)
        o_ref[t * tq:(t + 1) * tq, head_cols(hd)] = (o.T * sub_ln_gain * (1.0 - lam0)).astype(BF16)

    def emit_in_turn(*stages):
        pending = list(stages)
        while pending:
            for g in list(pending):
                if next(g, StopIteration) is StopIteration:
                    pending.remove(g)

    n_tiles = seq // tq
    n_sections = len(sections)
    for section in range(n_sections):
        emit_in_turn(project(0, section))
    units = [(hd, t, c) for t in range(n_tiles) for hd in range(HEADS_PER_STEP) for c in range(2)]
    assert len(units) // n_tiles == n_sections
    scores = {}
    emit_in_turn(score_pieces(*units[0], scores))
    done = {}
    for i, (hd, t, c) in enumerate(units):
        nxt = {}
        done[c] = {}
        stages = [score_pieces(*units[i + 1], nxt)] if i + 1 < len(units) else []
        stages.append(value_pieces(hd, t, scores, done[c]))
        if t + 1 < n_tiles:
            stages.append(project(t + 1, i % n_sections))
        emit_in_turn(*stages)
        scores = nxt
        if c == 1:
            finish_tile(hd, t, done[0], done[1])


def _proj_attn(x2d, g, w_in, lam_qk, subln_g, later_weights, layer, *, batch, seq, lam0):
    n, d = x2d.shape
    groups = N_DIFF_HEADS // HEADS_PER_STEP
    width = HEADS_PER_STEP * DIFF_V_DIM
    specs = [_slab_specs(w.shape[1], w.shape[2], batch * groups, layer,
                         lambda b, h: b * groups + h) for w in later_weights]

    def w_section(section):
        return pl.BlockSpec((None, d, width), lambda b, h: (layer, 0, section * groups + h))

    outs = pl.pallas_call(
        functools.partial(_proj_attn_kernel, lam0=lam0, layer=layer, n_cast=len(later_weights)),
        grid=(batch, groups),
        in_specs=[
            pl.BlockSpec((seq, d), lambda b, h: (b, 0)),
            _resident(g.shape),
            w_section(0), w_section(1), w_section(2), w_section(3),
            _layer_resident(lam_qk.shape[1:], layer),
            _resident(subln_g.shape),
            *(spec_in for spec_in, _ in specs),
        ],
        out_specs=[pl.BlockSpec((seq, width), lambda b, h: (b, h)),
                   pl.BlockSpec((seq, width), lambda b, h: (b, h)),
                   *(spec_out for _, spec_out in specs)],
        out_shape=[jax.ShapeDtypeStruct((n, ATTN_WIDTH), BF16),
                   jax.ShapeDtypeStruct((n, groups * width), BF16),
                   *(jax.ShapeDtypeStruct(w.shape[1:], BF16) for w in later_weights)],
        scratch_shapes=[
            pltpu.VMEM((seq, width), BF16),
            pltpu.VMEM((seq, width), BF16),
            pltpu.VMEM((HEADS_PER_STEP, DIFF_V_DIM, seq), BF16),
        ],
        compiler_params=pltpu.CompilerParams(
            dimension_semantics=("arbitrary", "arbitrary"), vmem_limit_bytes=VMEM_LIMIT_BYTES),
        name="proj_attention",
    )(x2d, g, w_in, w_in, w_in, w_in, lam_qk, subln_g, *later_weights)
    return outs[0], outs[1], outs[2:]


def _mix_ffn_kernel(x_ref, a_ref, u_ref, halo_ref, pw_ref, ps_ref, wo_ref, g2_ref,
                    wg_ref, wu_ref, wd_ref, gf_ref, o_ref, wpool_ref,
                    *, layer, tiles_per_seq, final_norm):
    tm = TOKEN_TILE
    attn_width = a_ref.shape[1]

    @pl.when(pl.program_id(0) == 0)
    def _():
        for g in range(len(POOL_WINDOWS)):
            rows = slice(g * POOL_GROUP_DIM, (g + 1) * POOL_GROUP_DIM)
            scaled = (pw_ref[rows, :].astype(F32) * ps_ref[layer:layer + 1, rows]).astype(BF16)
            wpool_ref[rows, :] = jnp.dot(
                scaled, wo_ref[attn_width + g * POOL_GROUP_DIM:attn_width + (g + 1) * POOL_GROUP_DIM, :],
                preferred_element_type=F32).astype(BF16)

    def token_tile(sub, carry):
        _mix_ffn_tile(sub, x_ref, a_ref, u_ref, halo_ref, wpool_ref, wo_ref, g2_ref, wg_ref, wu_ref,
                      wd_ref, gf_ref, o_ref, layer=layer, tiles_per_seq=tiles_per_seq,
                      final_norm=final_norm)
        return carry

    lax.fori_loop(0, x_ref.shape[0] // tm, token_tile, 0)


def _mix_ffn_tile(sub, x_ref, a_ref, u_ref, halo_ref, wpool_ref, wo_ref, g2_ref, wg_ref, wu_ref,
                  wd_ref, gf_ref, o_ref, *, layer, tiles_per_seq, final_norm):
    tm = TOKEN_TILE
    tiles_per_step = x_ref.shape[0] // tm
    attn_width = a_ref.shape[1]
    row0 = pl.multiple_of(sub * tm, tm)
    rows = pl.ds(row0, tm)
    t0 = ((pl.program_id(0) * tiles_per_step + sub) % tiles_per_seq) * tm

    d_model = x_ref.shape[1]
    col_step = d_model // len(POOL_WINDOWS)
    a_parts = []

    u = u_ref[rows, :].astype(F32)
    prev0 = pl.multiple_of(jnp.maximum(row0 - MAX_WINDOW, 0), MAX_WINDOW)
    halo = jnp.where(sub == 0, halo_ref[...], u_ref[pl.ds(prev0, MAX_WINDOW), :]).astype(F32)
    ext = jnp.concatenate([jnp.where(t0 == 0, jnp.zeros_like(halo), halo), u], axis=0)
    pos = t0 + lax.broadcasted_iota(jnp.int32, (tm, 1), 0)
    resids = []
    for g, w in enumerate(POOL_WINDOWS):
        a_parts.append(jnp.dot(a_ref[rows, :], wo_ref[:attn_width, g * col_step:(g + 1) * col_step],
                               preferred_element_type=F32))
        cols = slice(g * POOL_GROUP_DIM, (g + 1) * POOL_GROUP_DIM)
        wsum = ext[:, cols]
        span = 1
        while span < w:
            wsum = wsum + pltpu.roll(wsum, shift=span, axis=0)
            span *= 2
        inv_count = 1.0 / jnp.minimum(pos + 1, w).astype(F32)
        resids.append((wsum[MAX_WINDOW:] * inv_count - u[:, cols]).astype(BF16))
    x1 = (x_ref[rows, :] + jnp.concatenate(a_parts, axis=-1)
          + jnp.dot(jnp.concatenate(resids, axis=-1), wpool_ref[...], preferred_element_type=F32))

    h2 = (x1 * g2_ref[layer:layer + 1, :]).astype(BF16)
    scale = _rms_scale(x1)
    ffn = None
    f0 = 0
    for fc in FF_CHUNKS:
        acts = []
        for c0 in range(f0, f0 + fc, MXU_TILE):
            w_gu = jnp.concatenate([wg_ref[:, c0:c0 + MXU_TILE], wu_ref[:, c0:c0 + MXU_TILE]], axis=1)
            gu = jnp.dot(h2, w_gu, preferred_element_type=F32) * scale
            gate, up = gu[:, :MXU_TILE], gu[:, MXU_TILE:]
            acts.append((gate * (1.0 / (1.0 + jnp.exp(-gate))) * up).astype(BF16))
        act = jnp.concatenate(acts, axis=1)
        part = jnp.dot(act, wd_ref[f0:f0 + fc, :], preferred_element_type=F32)
        ffn = part if ffn is None else ffn + part
        f0 += fc
    x2 = x1 + ffn
    if final_norm:
        x2 = x2 * _rms_scale(x2) * gf_ref[...]
    o_ref[rows, :] = x2


def _mix_ffn(x2d, attn, u_all, pool_w, pool_scale, w_out, g2, w_gate, w_up, w_down, final_g,
             layer, *, seq, final_norm):
    n, d = x2d.shape
    d_ff = w_gate.shape[1]
    assert sum(FF_CHUNKS) == d_ff
    assert all(w & (w - 1) == 0 for w in POOL_WINDOWS), "doubling needs power-of-two windows"
    pool_width = len(POOL_WINDOWS) * POOL_GROUP_DIM
    tm = MIX_TILES_PER_STEP * TOKEN_TILE
    halo_per_tile = tm // MAX_WINDOW
    return pl.pallas_call(
        functools.partial(_mix_ffn_kernel, layer=layer, tiles_per_seq=seq // TOKEN_TILE,
                          final_norm=final_norm),
        grid=(n // tm,),
        in_specs=[
            pl.BlockSpec((tm, d), lambda i: (i, 0)),
            pl.BlockSpec((tm, attn.shape[1]), lambda i: (i, 0)),
            pl.BlockSpec((tm, pool_width), lambda i: (i, 0)),
            pl.BlockSpec((MAX_WINDOW, pool_width),
                         lambda i: (jnp.maximum(i * halo_per_tile - 1, 0), 0)),
            _resident(pool_w.shape),
            _resident(pool_scale.shape),
            _resident(w_out.shape),
            _resident(g2.shape),
            _resident(w_gate.shape),
            _resident(w_up.shape),
            _resident(w_down.shape),
            _resident((1, d)),
        ],
        out_specs=pl.BlockSpec((tm, d), lambda i: (i, 0)),
        out_shape=jax.ShapeDtypeStruct((n, d), F32),
        scratch_shapes=[pltpu.VMEM((pool_width, d), BF16)],
        compiler_params=pltpu.CompilerParams(
            dimension_semantics=("arbitrary",), vmem_limit_bytes=VMEM_LIMIT_BYTES),
        name="mix_ffn",
    )(x2d, attn, u_all, u_all, pool_w, pool_scale, w_out, g2, w_gate, w_up, w_down, final_g)


def kernel(x, norm1_g, w_in, lam_qk, subln_g, pool_w, pool_scale, w_out, norm2_g, w_gate, w_up,
           w_down, final_g):
    batch, seq, d = x.shape
    depth = w_in.shape[0]
    assert seq % (MIX_TILES_PER_STEP * TOKEN_TILE) == 0 and seq % Q_TILE == 0
    x2d = x.reshape(batch * seq, d)
    later = (pool_w.reshape(depth, -1, pool_w.shape[-1]), w_out, w_gate, w_up, w_down)
    final_g = final_g.reshape(1, d)
    for layer in range(depth):
        attn, u_all, (pw, wo, wg, wu, wd) = _proj_attn(
            x2d, norm1_g, w_in, lam_qk, subln_g, later, layer, batch=batch, seq=seq,
            lam0=_lambda_init(layer))
        x2d = _mix_ffn(x2d, attn, u_all, pw, pool_scale, wo, norm2_g, wg, wu, wd, final_g, layer,
                       seq=seq, final_norm=(layer == depth - 1))
    return x2d.reshape(batch, seq, d)
```

```python
import functools
import math

import jax
import jax.numpy as jnp
from jax import lax
from jax.experimental import pallas as pl
from jax.experimental.pallas import tpu as pltpu

F32 = jnp.float32
BF16 = jnp.bfloat16

N_DIFF_HEADS = 4
DIFF_HEAD_DIM = 64
DIFF_V_DIM = 2 * DIFF_HEAD_DIM
ATTN_WIDTH = N_DIFF_HEADS * DIFF_V_DIM
Q_SCALE = DIFF_HEAD_DIM ** -0.5 * math.log2(math.e)
POOL_WINDOWS = (2, 4, 8, 16)
POOL_GROUP_DIM = 128
MAX_WINDOW = max(POOL_WINDOWS)
NORM_EPS = 1e-5

F32_SUBLANES = 8
BF16_SUBLANE_TILE = 16
MXU_TILE = 256

VMEM_LIMIT_BYTES = 60 * 1024 * 1024

TOKEN_TILE = 512
MIX_TILES_PER_STEP = 2
Q_TILE = 512
KEY_BLOCK = 512
HEADS_PER_STEP = 2
FF_CHUNKS = (1536, 1280)


def _lambda_init(layer):
    return 0.8 - 0.6 * math.exp(-0.3 * layer)


def _rms_scale(x):
    return lax.rsqrt(jnp.mean(x * x, axis=-1, keepdims=True) + NORM_EPS)


def _resident(shape):
    return pl.BlockSpec(shape, lambda *_: (0,) * len(shape), pipeline_mode=pl.Buffered(1))


def _layer_resident(shape, layer):
    return pl.BlockSpec((None, *shape), lambda *_: (layer,) + (0,) * len(shape),
                        pipeline_mode=pl.Buffered(1))


def _slab_specs(rows, cols, n_steps, layer, step_of):
    hold = next(h for h in range(1, n_steps + 1)
                if n_steps % h == 0 and rows % (n_steps // h) == 0
                and (rows // (n_steps // h)) % BF16_SUBLANE_TILE == 0)
    slab = rows // (n_steps // hold)
    return (pl.BlockSpec((None, slab, cols), lambda *ids: (layer, step_of(*ids) // hold, 0)),
            pl.BlockSpec((slab, cols), lambda *ids: (step_of(*ids) // hold, 0)))


def _proj_attn_kernel(x_ref, g_ref, wq_ref, wk_ref, wv_ref, wu_ref, lamqk_ref, sg_ref, *rest,
                      lam0, layer, n_cast):
    cast_in = rest[:n_cast]
    o_ref, u_ref = rest[n_cast], rest[n_cast + 1]
    cast_out = rest[n_cast + 2:2 * n_cast + 2]
    q_ref, k_ref, vt_ref = rest[2 * n_cast + 2:]
    for src, dst in zip(cast_in, cast_out):
        dst[...] = src[...].astype(BF16)

    seq = x_ref.shape[0]
    tq = Q_TILE
    half = tq // 2

    sections = [w[...].astype(BF16) for w in (wq_ref, wk_ref, wv_ref, wu_ref)]

    def head_cols(hd):
        return slice(hd * DIFF_V_DIM, (hd + 1) * DIFF_V_DIM)

    normed = {}

    def normalised(t):
        if t not in normed:
            x = x_ref[t * tq:(t + 1) * tq, :]
            normed[t] = ((x * g_ref[layer:layer + 1, :]).astype(BF16), _rms_scale(x))
        return normed[t]

    def project(t, section):
        rows = slice(t * tq, (t + 1) * tq)
        h, scale = normalised(t)
        out = jnp.dot(h, sections[section], preferred_element_type=F32) * scale
        if section == 0:
            q_ref[rows, :] = (out * Q_SCALE).astype(BF16)
        elif section == 1:
            k_ref[rows, :] = out.astype(BF16)
        elif section == 2:
            for hd in range(HEADS_PER_STEP):
                vt_ref[hd, :, rows] = out[:, head_cols(hd)].T.astype(BF16)
        else:
            u_ref[rows, :] = out.astype(BF16)
        yield

    sub_ln_gain = sg_ref[layer:layer + 1, :]
    lq = lamqk_ref[...]
    lam = (jnp.exp(jnp.sum(lq[0:1] * lq[1:2], axis=-1, keepdims=True))
           - jnp.exp(jnp.sum(lq[2:3] * lq[3:4], axis=-1, keepdims=True)) + lam0)

    def scores_t(kb, qc):
        return lax.dot_general(kb, qc, (((1,), (1,)), ((), ())), preferred_element_type=F32)

    def fold8(x, op):
        return op(x.reshape(-1, F32_SUBLANES, x.shape[-1]), axis=0)

    def upper_cols(wide, narrow, op):
        return jnp.concatenate([wide[:, :half], op(wide[:, half:], narrow)], axis=1)

    keep_wide = (lax.broadcasted_iota(jnp.int32, (half, tq), 0)
                 <= lax.broadcasted_iota(jnp.int32, (half, tq), 1))
    keep_narrow = keep_wide[:, :half]

    def score_pieces(hd, t, c, out):
        q0 = t * tq
        q = q_ref[q0:q0 + tq, head_cols(hd)]
        lane = lax.broadcasted_iota(jnp.int32, q.shape, 1)
        mine = (lane < DIFF_HEAD_DIM) if c == 0 else (lane >= DIFF_HEAD_DIM)
        qc = jnp.where(mine, q, jnp.zeros_like(q))
        full, mpart = [], None
        for k0 in range(0, q0, KEY_BLOCK):
            k1 = min(k0 + KEY_BLOCK, q0)
            s = scores_t(k_ref[k0:k1, head_cols(hd)], qc)
            mblk = fold8(s, jnp.max)
            mpart = mblk if mpart is None else jnp.maximum(mpart, mblk)
            full.append((k0, k1, s))
            yield
        s_diag = jnp.where(keep_wide, scores_t(k_ref[q0:q0 + half, head_cols(hd)], qc), -jnp.inf)
        s_last = jnp.where(keep_narrow,
                           scores_t(k_ref[q0 + half:q0 + tq, head_cols(hd)], qc[half:]), -jnp.inf)
        mblk = fold8(s_diag, jnp.max)
        mpart = mblk if mpart is None else jnp.maximum(mpart, mblk)
        mpart = upper_cols(mpart, fold8(s_last, jnp.max), jnp.maximum)
        out.update(full=full, s_diag=s_diag, s_last=s_last,
                   m=jnp.max(mpart, axis=0, keepdims=True))
        yield

    def value_pieces(hd, t, scores, out):
        q0 = t * tq
        m = scores["m"]
        acc, lpart = None, None
        for k0, k1, s in scores["full"]:
            p = jnp.exp2(s - m)
            lblk = fold8(p, jnp.sum)
            lpart = lblk if lpart is None else lpart + lblk
            part = jnp.dot(vt_ref[hd, :, k0:k1], p.astype(BF16), preferred_element_type=F32)
            acc = part if acc is None else acc + part
            yield
        p_diag = jnp.exp2(scores["s_diag"] - m)
        p_last = jnp.exp2(scores["s_last"] - m[:, half:])
        lblk = fold8(p_diag, jnp.sum)
        lpart = lblk if lpart is None else lpart + lblk
        lpart = upper_cols(lpart, fold8(p_last, jnp.sum), jnp.add)
        part = jnp.dot(vt_ref[hd, :, q0:q0 + half], p_diag.astype(BF16), preferred_element_type=F32)
        acc = part if acc is None else acc + part
        acc = upper_cols(acc, jnp.dot(vt_ref[hd, :, q0 + half:q0 + tq], p_last.astype(BF16),
                                      preferred_element_type=F32), jnp.add)
        out.update(acc=acc, l=jnp.sum(lpart, axis=0, keepdims=True))
        yield

    def finish_tile(hd, t, comp1, comp2):
        o = comp1["acc"] * (1.0 / comp1["l"]) - comp2["acc"] * (lam / comp2["l"])
        o = o * lax.rsqrt(jnp.mean(o * o, axis=0, keepdims=True) + NORM_EPS)
        o_ref[t * tq:(t + 1) * tq, head_cols(hd)] = (o.T * sub_ln_gain * (1.0 - lam0)).astype(BF16)

    def emit_in_turn(*stages):
        pending = list(stages)
        while pending:
            for g in list(pending):
                if next(g, StopIteration) is StopIteration:
                    pending.remove(g)

    n_tiles = seq // tq
    n_sections = len(sections)
    for section in range(n_sections):
        emit_in_turn(project(0, section))
    units = [(hd, t, c) for hd in range(HEADS_PER_STEP) for t in range(n_tiles) for c in range(2)]
    assert n_sections == 4
    scores = {}
    emit_in_turn(score_pieces(*units[0], scores))
    done = {}
    for i, (hd, t, c) in enumerate(units):
        nxt = {}
        done[c] = {}
        stages = [score_pieces(*units[i + 1], nxt)] if i + 1 < len(units) else []
        stages.append(value_pieces(hd, t, scores, done[c]))
        if hd == 0 and t + 1 < n_tiles:
            stages += [project(t + 1, 2 * c), project(t + 1, 2 * c + 1)]
        emit_in_turn(*stages)
        scores = nxt
        if c == 1:
            finish_tile(hd, t, done[0], done[1])


def _proj_attn(x2d, g, w_in, lam_qk, subln_g, later_weights, layer, *, batch, seq, lam0):
    n, d = x2d.shape
    groups = N_DIFF_HEADS // HEADS_PER_STEP
    width = HEADS_PER_STEP * DIFF_V_DIM
    specs = [_slab_specs(w.shape[1], w.shape[2], batch * groups, layer,
                         lambda b, h: b * groups + h) for w in later_weights]

    def w_section(section):
        return pl.BlockSpec((None, d, width), lambda b, h: (layer, 0, section * groups + h))

    outs = pl.pallas_call(
        functools.partial(_proj_attn_kernel, lam0=lam0, layer=layer, n_cast=len(later_weights)),
        grid=(batch, groups),
        in_specs=[
            pl.BlockSpec((seq, d), lambda b, h: (b, 0)),
            _resident(g.shape),
            w_section(0), w_section(1), w_section(2), w_section(3),
            _layer_resident(lam_qk.shape[1:], layer),
            _resident(subln_g.shape),
            *(spec_in for spec_in, _ in specs),
        ],
        out_specs=[pl.BlockSpec((seq, width), lambda b, h: (b, h)),
                   pl.BlockSpec((seq, width), lambda b, h: (b, h)),
                   *(spec_out for _, spec_out in specs)],
        out_shape=[jax.ShapeDtypeStruct((n, ATTN_WIDTH), BF16),
                   jax.ShapeDtypeStruct((n, groups * width), BF16),
                   *(jax.ShapeDtypeStruct(w.shape[1:], BF16) for w in later_weights)],
        scratch_shapes=[
            pltpu.VMEM((seq, width), BF16),
            pltpu.VMEM((seq, width), BF16),
            pltpu.VMEM((HEADS_PER_STEP, DIFF_V_DIM, seq), BF16),
        ],
        compiler_params=pltpu.CompilerParams(
            dimension_semantics=("arbitrary", "arbitrary"), vmem_limit_bytes=VMEM_LIMIT_BYTES),
        name="proj_attention",
    )(x2d, g, w_in, w_in, w_in, w_in, lam_qk, subln_g, *later_weights)
    return outs[0], outs[1], outs[2:]


def _mix_ffn_kernel(x_ref, a_ref, u_ref, halo_ref, pw_ref, ps_ref, wo_ref, g2_ref,
                    wg_ref, wu_ref, wd_ref, gf_ref, o_ref, wpool_ref,
                    *, layer, tiles_per_seq, final_norm):
    tm = TOKEN_TILE
    attn_width = a_ref.shape[1]

    @pl.when(pl.program_id(0) == 0)
    def _():
        for g in range(len(POOL_WINDOWS)):
            rows = slice(g * POOL_GROUP_DIM, (g + 1) * POOL_GROUP_DIM)
            scaled = (pw_ref[rows, :].astype(F32) * ps_ref[layer:layer + 1, rows]).astype(BF16)
            wpool_ref[rows, :] = jnp.dot(
                scaled, wo_ref[attn_width + g * POOL_GROUP_DIM:attn_width + (g + 1) * POOL_GROUP_DIM, :],
                preferred_element_type=F32).astype(BF16)

    def token_tile(sub, carry):
        _mix_ffn_tile(sub, x_ref, a_ref, u_ref, halo_ref, wpool_ref, wo_ref, g2_ref, wg_ref, wu_ref,
                      wd_ref, gf_ref, o_ref, layer=layer, tiles_per_seq=tiles_per_seq,
                      final_norm=final_norm)
        return carry

    lax.fori_loop(0, x_ref.shape[0] // tm, token_tile, 0)


def _mix_ffn_tile(sub, x_ref, a_ref, u_ref, halo_ref, wpool_ref, wo_ref, g2_ref, wg_ref, wu_ref,
                  wd_ref, gf_ref, o_ref, *, layer, tiles_per_seq, final_norm):
    tm = TOKEN_TILE
    tiles_per_step = x_ref.shape[0] // tm
    attn_width = a_ref.shape[1]
    row0 = pl.multiple_of(sub * tm, tm)
    rows = pl.ds(row0, tm)
    t0 = ((pl.program_id(0) * tiles_per_step + sub) % tiles_per_seq) * tm

    d_model = x_ref.shape[1]
    col_step = d_model // len(POOL_WINDOWS)
    a_parts = []

    u = u_ref[rows, :].astype(F32)
    prev0 = pl.multiple_of(jnp.maximum(row0 - MAX_WINDOW, 0), MAX_WINDOW)
    halo = jnp.where(sub == 0, halo_ref[...], u_ref[pl.ds(prev0, MAX_WINDOW), :]).astype(F32)
    ext = jnp.concatenate([jnp.where(t0 == 0, jnp.zeros_like(halo), halo), u], axis=0)
    pos = t0 + lax.broadcasted_iota(jnp.int32, (tm, 1), 0)
    resids = []
    for g, w in enumerate(POOL_WINDOWS):
        a_parts.append(jnp.dot(a_ref[rows, :], wo_ref[:attn_width, g * col_step:(g + 1) * col_step],
                               preferred_element_type=F32))
        cols = slice(g * POOL_GROUP_DIM, (g + 1) * POOL_GROUP_DIM)
        wsum = ext[:, cols]
        span = 1
        while span < w:
            wsum = wsum + pltpu.roll(wsum, shift=span, axis=0)
            span *= 2
        inv_count = 1.0 / jnp.minimum(pos + 1, w).astype(F32)
        resids.append((wsum[MAX_WINDOW:] * inv_count - u[:, cols]).astype(BF16))
    x1 = (x_ref[rows, :] + jnp.concatenate(a_parts, axis=-1)
          + jnp.dot(jnp.concatenate(resids, axis=-1), wpool_ref[...], preferred_element_type=F32))

    h2 = (x1 * g2_ref[layer:layer + 1, :]).astype(BF16)
    scale = _rms_scale(x1)
    ffn = None
    f0 = 0
    for fc in FF_CHUNKS:
        acts = []
        for c0 in range(f0, f0 + fc, MXU_TILE):
            w_gu = jnp.concatenate([wg_ref[:, c0:c0 + MXU_TILE], wu_ref[:, c0:c0 + MXU_TILE]], axis=1)
            gu = jnp.dot(h2, w_gu, preferred_element_type=F32) * scale
            gate, up = gu[:, :MXU_TILE], gu[:, MXU_TILE:]
            acts.append((gate * (1.0 / (1.0 + jnp.exp(-gate))) * up).astype(BF16))
        act = jnp.concatenate(acts, axis=1)
        part = jnp.dot(act, wd_ref[f0:f0 + fc, :], preferred_element_type=F32)
        ffn = part if ffn is None else ffn + part
        f0 += fc
    x2 = x1 + ffn
    if final_norm:
        x2 = x2 * _rms_scale(x2) * gf_ref[...]
    o_ref[rows, :] = x2


def _mix_ffn(x2d, attn, u_all, pool_w, pool_scale, w_out, g2, w_gate, w_up, w_down, final_g,
             layer, *, seq, final_norm):
    n, d = x2d.shape
    d_ff = w_gate.shape[1]
    assert sum(FF_CHUNKS) == d_ff
    assert all(w & (w - 1) == 0 for w in POOL_WINDOWS), "doubling needs power-of-two windows"
    pool_width = len(POOL_WINDOWS) * POOL_GROUP_DIM
    tm = MIX_TILES_PER_STEP * TOKEN_TILE
    halo_per_tile = tm // MAX_WINDOW
    return pl.pallas_call(
        functools.partial(_mix_ffn_kernel, layer=layer, tiles_per_seq=seq // TOKEN_TILE,
                          final_norm=final_norm),
        grid=(n // tm,),
        in_specs=[
            pl.BlockSpec((tm, d), lambda i: (i, 0)),
            pl.BlockSpec((tm, attn.shape[1]), lambda i: (i, 0)),
            pl.BlockSpec((tm, pool_width), lambda i: (i, 0)),
            pl.BlockSpec((MAX_WINDOW, pool_width),
                         lambda i: (jnp.maximum(i * halo_per_tile - 1, 0), 0)),
            _resident(pool_w.shape),
            _resident(pool_scale.shape),
            _resident(w_out.shape),
            _resident(g2.shape),
            _resident(w_gate.shape),
            _resident(w_up.shape),
            _resident(w_down.shape),
            _resident((1, d)),
        ],
        out_specs=pl.BlockSpec((tm, d), lambda i: (i, 0)),
        out_shape=jax.ShapeDtypeStruct((n, d), F32),
        scratch_shapes=[pltpu.VMEM((pool_width, d), BF16)],
        compiler_params=pltpu.CompilerParams(
            dimension_semantics=("arbitrary",), vmem_limit_bytes=VMEM_LIMIT_BYTES),
        name="mix_ffn",
    )(x2d, attn, u_all, u_all, pool_w, pool_scale, w_out, g2, w_gate, w_up, w_down, final_g)


def kernel(x, norm1_g, w_in, lam_qk, subln_g, pool_w, pool_scale, w_out, norm2_g, w_gate, w_up,
           w_down, final_g):
    batch, seq, d = x.shape
    depth = w_in.shape[0]
    assert seq % (MIX_TILES_PER_STEP * TOKEN_TILE) == 0 and seq % Q_TILE == 0
    x2d = x.reshape(batch * seq, d)
    later = (pool_w.reshape(depth, -1, pool_w.shape[-1]), w_out, w_gate, w_up, w_down)
    final_g = final_g.reshape(1, d)
    for layer in range(depth):
        attn, u_all, (pw, wo, wg, wu, wd) = _proj_attn(
            x2d, norm1_g, w_in, lam_qk, subln_g, later, layer, batch=batch, seq=seq,
            lam0=_lambda_init(layer))
        x2d = _mix_ffn(x2d, attn, u_all, pw, pool_scale, wo, norm2_g, wg, wu, wd, final_g, layer,
                       seq=seq, final_norm=(layer == depth - 1))
    return x2d.reshape(batch, seq, d)
```

```python
import functools
import math

import jax
import jax.numpy as jnp
from jax import lax
from jax.experimental import pallas as pl
from jax.experimental.pallas import tpu as pltpu

F32 = jnp.float32
BF16 = jnp.bfloat16

N_DIFF_HEADS = 4
DIFF_HEAD_DIM = 64
DIFF_V_DIM = 2 * DIFF_HEAD_DIM
ATTN_WIDTH = N_DIFF_HEADS * DIFF_V_DIM
Q_SCALE = DIFF_HEAD_DIM ** -0.5 * math.log2(math.e)
POOL_WINDOWS = (2, 4, 8, 16)
POOL_GROUP_DIM = 128
MAX_WINDOW = max(POOL_WINDOWS)
NORM_EPS = 1e-5

F32_SUBLANES = 8
BF16_SUBLANE_TILE = 16
MXU_TILE = 256

VMEM_LIMIT_BYTES = 60 * 1024 * 1024

TOKEN_TILE = 512
MIX_TILES_PER_STEP = 2
Q_TILE = 512
KEY_BLOCK = 512
HEADS_PER_STEP = 2
FF_CHUNKS = (1536, 1280)


def _lambda_init(layer):
    return 0.8 - 0.6 * math.exp(-0.3 * layer)


def _rms_scale(x):
    return lax.rsqrt(jnp.mean(x * x, axis=-1, keepdims=True) + NORM_EPS)


def _resident(shape):
    return pl.BlockSpec(shape, lambda *_: (0,) * len(shape), pipeline_mode=pl.Buffered(1))


def _layer_resident(shape, layer):
    return pl.BlockSpec((None, *shape), lambda *_: (layer,) + (0,) * len(shape),
                        pipeline_mode=pl.Buffered(1))


def _slab_specs(rows, cols, n_steps, layer, step_of):
    hold = next(h for h in range(1, n_steps + 1)
                if n_steps % h == 0 and rows % (n_steps // h) == 0
                and (rows // (n_steps // h)) % BF16_SUBLANE_TILE == 0)
    slab = rows // (n_steps // hold)
    return (pl.BlockSpec((None, slab, cols), lambda *ids: (layer, step_of(*ids) // hold, 0)),
            pl.BlockSpec((slab, cols), lambda *ids: (step_of(*ids) // hold, 0)))


def _proj_attn_kernel(x_ref, g_ref, wq_ref, wk_ref, wv_ref, wu_ref, lamqk_ref, sg_ref, *rest,
                      lam0, layer, n_cast):
    cast_in = rest[:n_cast]
    o_ref, u_ref = rest[n_cast], rest[n_cast + 1]
    cast_out = rest[n_cast + 2:2 * n_cast + 2]
    q_ref, k_ref, vt_ref = rest[2 * n_cast + 2:]
    for src, dst in zip(cast_in, cast_out):
        dst[...] = src[...].astype(BF16)

    seq = x_ref.shape[0]
    tq = Q_TILE
    half = tq // 2

    sections = [w[...].astype(BF16) for w in (wq_ref, wk_ref, wv_ref, wu_ref)]

    def head_cols(hd):
        return slice(hd * DIFF_V_DIM, (hd + 1) * DIFF_V_DIM)

    normed = {}

    def normalised(t):
        if t not in normed:
            x = x_ref[t * tq:(t + 1) * tq, :]
            normed[t] = ((x * g_ref[layer:layer + 1, :]).astype(BF16), _rms_scale(x))
        return normed[t]

    def project(t, section):
        rows = slice(t * tq, (t + 1) * tq)
        h, scale = normalised(t)
        out = jnp.dot(h, sections[section], preferred_element_type=F32) * scale
        if section == 0:
            q_ref[rows, :] = (out * Q_SCALE).astype(BF16)
        elif section == 1:
            k_ref[rows, :] = out.astype(BF16)
        elif section == 2:
            for hd in range(HEADS_PER_STEP):
                vt_ref[hd, :, rows] = out[:, head_cols(hd)].T.astype(BF16)
        else:
            u_ref[rows, :] = out.astype(BF16)
        yield

    sub_ln_gain = sg_ref[layer:layer + 1, :]
    lq = lamqk_ref[...]
    lam = (jnp.exp(jnp.sum(lq[0:1] * lq[1:2], axis=-1, keepdims=True))
           - jnp.exp(jnp.sum(lq[2:3] * lq[3:4], axis=-1, keepdims=True)) + lam0)

    def scores_t(kb, qc):
        return lax.dot_general(kb, qc, (((1,), (1,)), ((), ())), preferred_element_type=F32)

    def fold8(x, op):
        return op(x.reshape(-1, F32_SUBLANES, x.shape[-1]), axis=0)

    def upper_cols(wide, narrow, op):
        return jnp.concatenate([wide[:, :half], op(wide[:, half:], narrow)], axis=1)

    keep_wide = (lax.broadcasted_iota(jnp.int32, (half, tq), 0)
                 <= lax.broadcasted_iota(jnp.int32, (half, tq), 1))
    keep_narrow = keep_wide[:, :half]

    def score_pieces(hd, t, c, out):
        q0 = t * tq
        q = q_ref[q0:q0 + tq, head_cols(hd)]
        lane = lax.broadcasted_iota(jnp.int32, q.shape, 1)
        mine = (lane < DIFF_HEAD_DIM) if c == 0 else (lane >= DIFF_HEAD_DIM)
        qc = jnp.where(mine, q, jnp.zeros_like(q))
        full, mpart = [], None
        for k0 in range(0, q0, KEY_BLOCK):
            k1 = min(k0 + KEY_BLOCK, q0)
            s = scores_t(k_ref[k0:k1, head_cols(hd)], qc)
            mblk = fold8(s, jnp.max)
            mpart = mblk if mpart is None else jnp.maximum(mpart, mblk)
            full.append((k0, k1, s))
            yield
        s_diag = jnp.where(keep_wide, scores_t(k_ref[q0:q0 + half, head_cols(hd)], qc), -jnp.inf)
        s_last = jnp.where(keep_narrow,
                           scores_t(k_ref[q0 + half:q0 + tq, head_cols(hd)], qc[half:]), -jnp.inf)
        mblk = fold8(s_diag, jnp.max)
        mpart = mblk if mpart is None else jnp.maximum(mpart, mblk)
        mpart = upper_cols(mpart, fold8(s_last, jnp.max), jnp.maximum)
        out.update(full=full, s_diag=s_diag, s_last=s_last,
                   m=jnp.max(mpart, axis=0, keepdims=True))
        yield

    def value_pieces(hd, t, scores, out):
        q0 = t * tq
        m = scores["m"]
        acc, lpart = None, None
        for k0, k1, s in scores["full"]:
            p = jnp.exp2(s - m)
            lblk = fold8(p, jnp.sum)
            lpart = lblk if lpart is None else lpart + lblk
            part = jnp.dot(vt_ref[hd, :, k0:k1], p.astype(BF16), preferred_element_type=F32)
            acc = part if acc is None else acc + part
            yield
        p_diag = jnp.exp2(scores["s_diag"] - m)
        p_last = jnp.exp2(scores["s_last"] - m[:, half:])
        lblk = fold8(p_diag, jnp.sum)
        lpart = lblk if lpart is None else lpart + lblk
        lpart = upper_cols(lpart, fold8(p_last, jnp.sum), jnp.add)
        part = jnp.dot(vt_ref[hd, :, q0:q0 + half], p_diag.astype(BF16), preferred_element_type=F32)
        acc = part if acc is None else acc + part
        acc = upper_cols(acc, jnp.dot(vt_ref[hd, :, q0 + half:q0 + tq], p_last.astype(BF16),
                                      preferred_element_type=F32), jnp.add)
        out.update(acc=acc, l=jnp.sum(lpart, axis=0, keepdims=True))
        yield

    def finish_tile(hd, t, comp1, comp2):
        o = comp1["acc"] * (1.0 / comp1["l"]) - comp2["acc"] * (lam / comp2["l"])
        o = o * lax.rsqrt(jnp.mean(o * o, axis=0, keepdims=True) + NORM_EPS)
        o_ref[t * tq:(t + 1) * tq, head_cols(hd)] = (o.T * sub_ln_gain * (1.0 - lam0)).astype(BF16)

    def emit_in_turn(*stages):
        pending = list(stages)
        while pending:
            for g in list(pending):
                if next(g, StopIteration) is StopIteration:
                    pending.remove(g)

    n_tiles = seq // tq
    n_sections = len(sections)
    for section in range(n_sections):
        emit_in_turn(project(0, section))
    units = [(hd, t, c) for t in range(n_tiles) for hd in range(HEADS_PER_STEP) for c in range(2)]
    assert len(units) // n_tiles == n_sections
    scores = {}
    emit_in_turn(score_pieces(*units[0], scores))
    done = {}
    for i, (hd, t, c) in enumerate(units):
        nxt = {}
        done[c] = {}
        stages = [score_pieces(*units[i + 1], nxt)] if i + 1 < len(units) else []
        stages.append(value_pieces(hd, t, scores, done[c]))
        if t + 1 < n_tiles:
            stages.append(project(t + 1, i % n_sections))
        emit_in_turn(*stages)
        scores = nxt
        if c == 1:
            finish_tile(hd, t, done[0], done[1])


def _proj_attn(x2d, g, w_in, lam_qk, subln_g, later_weights, layer, *, batch, seq, lam0):
    n, d = x2d.shape
    groups = N_DIFF_HEADS // HEADS_PER_STEP
    width = HEADS_PER_STEP * DIFF_V_DIM
    specs = [_slab_specs(w.shape[1], w.shape[2], batch * groups, layer,
                         lambda h, b: h * batch + b) for w in later_weights]

    def w_section(section):
        return pl.BlockSpec((None, d, width), lambda h, b: (layer, 0, section * groups + h))

    outs = pl.pallas_call(
        functools.partial(_proj_attn_kernel, lam0=lam0, layer=layer, n_cast=len(later_weights)),
        grid=(groups, batch),
        in_specs=[
            pl.BlockSpec((seq, d), lambda h, b: (b, 0)),
            _resident(g.shape),
            w_section(0), w_section(1), w_section(2), w_section(3),
            _layer_resident(lam_qk.shape[1:], layer),
            _resident(subln_g.shape),
            *(spec_in for spec_in, _ in specs),
        ],
        out_specs=[pl.BlockSpec((seq, width), lambda h, b: (b, h)),
                   pl.BlockSpec((seq, width), lambda h, b: (b, h)),
                   *(spec_out for _, spec_out in specs)],
        out_shape=[jax.ShapeDtypeStruct((n, ATTN_WIDTH), BF16),
                   jax.ShapeDtypeStruct((n, groups * width), BF16),
                   *(jax.ShapeDtypeStruct(w.shape[1:], BF16) for w in later_weights)],
        scratch_shapes=[
            pltpu.VMEM((seq, width), BF16),
            pltpu.VMEM((seq, width), BF16),
            pltpu.VMEM((HEADS_PER_STEP, DIFF_V_DIM, seq), BF16),
        ],
        compiler_params=pltpu.CompilerParams(
            dimension_semantics=("arbitrary", "arbitrary"), vmem_limit_bytes=VMEM_LIMIT_BYTES),
        name="proj_attention",
    )(x2d, g, w_in, w_in, w_in, w_in, lam_qk, subln_g, *later_weights)
    return outs[0], outs[1], outs[2:]


def _mix_ffn_kernel(x_ref, a_ref, u_ref, halo_ref, pw_ref, ps_ref, wo_ref, g2_ref,
                    wg_ref, wu_ref, wd_ref, gf_ref, o_ref, wpool_ref,
                    *, layer, tiles_per_seq, final_norm):
    tm = TOKEN_TILE
    attn_width = a_ref.shape[1]

    @pl.when(pl.program_id(0) == 0)
    def _():
        for g in range(len(POOL_WINDOWS)):
            rows = slice(g * POOL_GROUP_DIM, (g + 1) * POOL_GROUP_DIM)
            scaled = (pw_ref[rows, :].astype(F32) * ps_ref[layer:layer + 1, rows]).astype(BF16)
            wpool_ref[rows, :] = jnp.dot(
                scaled, wo_ref[attn_width + g * POOL_GROUP_DIM:attn_width + (g + 1) * POOL_GROUP_DIM, :],
                preferred_element_type=F32).astype(BF16)

    def token_tile(sub, carry):
        _mix_ffn_tile(sub, x_ref, a_ref, u_ref, halo_ref, wpool_ref, wo_ref, g2_ref, wg_ref, wu_ref,
                      wd_ref, gf_ref, o_ref, layer=layer, tiles_per_seq=tiles_per_seq,
                      final_norm=final_norm)
        return carry

    lax.fori_loop(0, x_ref.shape[0] // tm, token_tile, 0)


def _mix_ffn_tile(sub, x_ref, a_ref, u_ref, halo_ref, wpool_ref, wo_ref, g2_ref, wg_ref, wu_ref,
                  wd_ref, gf_ref, o_ref, *, layer, tiles_per_seq, final_norm):
    tm = TOKEN_TILE
    tiles_per_step = x_ref.shape[0] // tm
    attn_width = a_ref.shape[1]
    row0 = pl.multiple_of(sub * tm, tm)
    rows = pl.ds(row0, tm)
    t0 = ((pl.program_id(0) * tiles_per_step + sub) % tiles_per_seq) * tm

    d_model = x_ref.shape[1]
    col_step = d_model // len(POOL_WINDOWS)
    a_parts = []

    u = u_ref[rows, :].astype(F32)
    prev0 = pl.multiple_of(jnp.maximum(row0 - MAX_WINDOW, 0), MAX_WINDOW)
    halo = jnp.where(sub == 0, halo_ref[...], u_ref[pl.ds(prev0, MAX_WINDOW), :]).astype(F32)
    ext = jnp.concatenate([jnp.where(t0 == 0, jnp.zeros_like(halo), halo), u], axis=0)
    pos = t0 + lax.broadcasted_iota(jnp.int32, (tm, 1), 0)
    resids = []
    for g, w in enumerate(POOL_WINDOWS):
        a_parts.append(jnp.dot(a_ref[rows, :], wo_ref[:attn_width, g * col_step:(g + 1) * col_step],
                               preferred_element_type=F32))
        cols = slice(g * POOL_GROUP_DIM, (g + 1) * POOL_GROUP_DIM)
        wsum = ext[:, cols]
        span = 1
        while span < w:
            wsum = wsum + pltpu.roll(wsum, shift=span, axis=0)
            span *= 2
        inv_count = 1.0 / jnp.minimum(pos + 1, w).astype(F32)
        resids.append((wsum[MAX_WINDOW:] * inv_count - u[:, cols]).astype(BF16))
    x1 = (x_ref[rows, :] + jnp.concatenate(a_parts, axis=-1)
          + jnp.dot(jnp.concatenate(resids, axis=-1), wpool_ref[...], preferred_element_type=F32))

    h2 = (x1 * g2_ref[layer:layer + 1, :]).astype(BF16)
    scale = _rms_scale(x1)
    ffn = None
    f0 = 0
    for fc in FF_CHUNKS:
        acts = []
        for c0 in range(f0, f0 + fc, MXU_TILE):
            w_gu = jnp.concatenate([wg_ref[:, c0:c0 + MXU_TILE], wu_ref[:, c0:c0 + MXU_TILE]], axis=1)
            gu = jnp.dot(h2, w_gu, preferred_element_type=F32) * scale
            gate, up = gu[:, :MXU_TILE], gu[:, MXU_TILE:]
            acts.append((gate * (1.0 / (1.0 + jnp.exp(-gate))) * up).astype(BF16))
        act = jnp.concatenate(acts, axis=1)
        part = jnp.dot(act, wd_ref[f0:f0 + fc, :], preferred_element_type=F32)
        ffn = part if ffn is None else ffn + part
        f0 += fc
    x2 = x1 + ffn
    if final_norm:
        x2 = x2 * _rms_scale(x2) * gf_ref[...]
    o_ref[rows, :] = x2


def _mix_ffn(x2d, attn, u_all, pool_w, pool_scale, w_out, g2, w_gate, w_up, w_down, final_g,
             layer, *, seq, final_norm):
    n, d = x2d.shape
    d_ff = w_gate.shape[1]
    assert sum(FF_CHUNKS) == d_ff
    assert all(w & (w - 1) == 0 for w in POOL_WINDOWS), "doubling needs power-of-two windows"
    pool_width = len(POOL_WINDOWS) * POOL_GROUP_DIM
    tm = MIX_TILES_PER_STEP * TOKEN_TILE
    halo_per_tile = tm // MAX_WINDOW
    return pl.pallas_call(
        functools.partial(_mix_ffn_kernel, layer=layer, tiles_per_seq=seq // TOKEN_TILE,
                          final_norm=final_norm),
        grid=(n // tm,),
        in_specs=[
            pl.BlockSpec((tm, d), lambda i: (i, 0)),
            pl.BlockSpec((tm, attn.shape[1]), lambda i: (i, 0)),
            pl.BlockSpec((tm, pool_width), lambda i: (i, 0)),
            pl.BlockSpec((MAX_WINDOW, pool_width),
                         lambda i: (jnp.maximum(i * halo_per_tile - 1, 0), 0)),
            _resident(pool_w.shape),
            _resident(pool_scale.shape),
            _resident(w_out.shape),
            _resident(g2.shape),
            _resident(w_gate.shape),
            _resident(w_up.shape),
            _resident(w_down.shape),
            _resident((1, d)),
        ],
        out_specs=pl.BlockSpec((tm, d), lambda i: (i, 0)),
        out_shape=jax.ShapeDtypeStruct((n, d), F32),
        scratch_shapes=[pltpu.VMEM((pool_width, d), BF16)],
        compiler_params=pltpu.CompilerParams(
            dimension_semantics=("arbitrary",), vmem_limit_bytes=VMEM_LIMIT_BYTES),
        name="mix_ffn",
    )(x2d, attn, u_all, u_all, pool_w, pool_scale, w_out, g2, w_gate, w_up, w_down, final_g)


def kernel(x, norm1_g, w_in, lam_qk, subln_g, pool_w, pool_scale, w_out, norm2_g, w_gate, w_up,
           w_down, final_g):
    batch, seq, d = x.shape
    depth = w_in.shape[0]
    assert seq % (MIX_TILES_PER_STEP * TOKEN_TILE) == 0 and seq % Q_TILE == 0
    x2d = x.reshape(batch * seq, d)
    later = (pool_w.reshape(depth, -1, pool_w.shape[-1]), w_out, w_gate, w_up, w_down)
    final_g = final_g.reshape(1, d)
    for layer in range(depth):
        attn, u_all, (pw, wo, wg, wu, wd) = _proj_attn(
            x2d, norm1_g, w_in, lam_qk, subln_g, later, layer, batch=batch, seq=seq,
            lam0=_lambda_init(layer))
        x2d = _mix_ffn(x2d, attn, u_all, pw, pool_scale, wo, norm2_g, wg, wu, wd, final_g, layer,
                       seq=seq, final_norm=(layer == depth - 1))
    return x2d.reshape(batch, seq, d)
```

```python
import functools
import math

import jax
import jax.numpy as jnp
from jax import lax
from jax.experimental import pallas as pl
from jax.experimental.pallas import tpu as pltpu

F32 = jnp.float32
BF16 = jnp.bfloat16

N_DIFF_HEADS = 4
DIFF_HEAD_DIM = 64
DIFF_V_DIM = 2 * DIFF_HEAD_DIM
ATTN_WIDTH = N_DIFF_HEADS * DIFF_V_DIM
Q_SCALE = DIFF_HEAD_DIM ** -0.5 * math.log2(math.e)
POOL_WINDOWS = (2, 4, 8, 16)
POOL_GROUP_DIM = 128
MAX_WINDOW = max(POOL_WINDOWS)
NORM_EPS = 1e-5

F32_SUBLANES = 8
BF16_SUBLANE_TILE = 16
MXU_TILE = 256

VMEM_LIMIT_BYTES = 60 * 1024 * 1024

TOKEN_TILE = 512
MIX_TILES_PER_STEP = 2
Q_TILE = 512
KEY_BLOCK = 512
HEADS_PER_STEP = 2
FF_CHUNKS = (1536, 1280)


def _lambda_init(layer):
    return 0.8 - 0.6 * math.exp(-0.3 * layer)


def _rms_scale(x):
    return lax.rsqrt(jnp.mean(x * x, axis=-1, keepdims=True) + NORM_EPS)


def _resident(shape):
    return pl.BlockSpec(shape, lambda *_: (0,) * len(shape), pipeline_mode=pl.Buffered(1))


def _layer_resident(shape, layer):
    return pl.BlockSpec((None, *shape), lambda *_: (layer,) + (0,) * len(shape),
                        pipeline_mode=pl.Buffered(1))


def _slab_specs(rows, cols, n_steps, layer, step_of):
    hold = next(h for h in range(1, n_steps + 1)
                if n_steps % h == 0 and rows % (n_steps // h) == 0
                and (rows // (n_steps // h)) % BF16_SUBLANE_TILE == 0)
    slab = rows // (n_steps // hold)
    return (pl.BlockSpec((None, slab, cols), lambda *ids: (layer, step_of(*ids) // hold, 0)),
            pl.BlockSpec((slab, cols), lambda *ids: (step_of(*ids) // hold, 0)))


def _proj_attn_kernel(x_ref, g_ref, wq_ref, wk_ref, wv_ref, wu_ref, lamqk_ref, sg_ref, *rest,
                      lam0, layer, n_cast):
    cast_in = rest[:n_cast]
    o_ref, u_ref = rest[n_cast], rest[n_cast + 1]
    cast_out = rest[n_cast + 2:2 * n_cast + 2]
    q_ref, k_ref, vt_ref = rest[2 * n_cast + 2:]
    for src, dst in zip(cast_in, cast_out):
        dst[...] = src[...].astype(BF16)

    seq = x_ref.shape[0]
    tq = Q_TILE
    half = tq // 2

    sections = [w[...].astype(BF16) for w in (wq_ref, wk_ref, wv_ref, wu_ref)]

    def head_cols(hd):
        return slice(hd * DIFF_V_DIM, (hd + 1) * DIFF_V_DIM)

    normed = {}

    def normalised(t):
        if t not in normed:
            x = x_ref[t * tq:(t + 1) * tq, :]
            normed[t] = ((x * g_ref[layer:layer + 1, :]).astype(BF16), _rms_scale(x))
        return normed[t]

    def project(t, section):
        rows = slice(t * tq, (t + 1) * tq)
        h, scale = normalised(t)
        out = jnp.dot(h, sections[section], preferred_element_type=F32) * scale
        if section == 0:
            q_ref[rows, :] = (out * Q_SCALE).astype(BF16)
        elif section == 1:
            k_ref[rows, :] = out.astype(BF16)
        elif section == 2:
            for hd in range(HEADS_PER_STEP):
                vt_ref[hd, :, rows] = out[:, head_cols(hd)].T.astype(BF16)
        else:
            u_ref[rows, :] = out.astype(BF16)
        yield

    sub_ln_gain = sg_ref[layer:layer + 1, :]
    lq = lamqk_ref[...]
    lam = (jnp.exp(jnp.sum(lq[0:1] * lq[1:2], axis=-1, keepdims=True))
           - jnp.exp(jnp.sum(lq[2:3] * lq[3:4], axis=-1, keepdims=True)) + lam0)

    def scores_t(kb, qc):
        return lax.dot_general(kb, qc, (((1,), (1,)), ((), ())), preferred_element_type=F32)

    def fold8(x, op):
        return op(x.reshape(-1, F32_SUBLANES, x.shape[-1]), axis=0)

    def upper_cols(wide, narrow, op):
        return jnp.concatenate([wide[:, :half], op(wide[:, half:], narrow)], axis=1)

    keep_wide = (lax.broadcasted_iota(jnp.int32, (half, tq), 0)
                 <= lax.broadcasted_iota(jnp.int32, (half, tq), 1))
    keep_narrow = keep_wide[:, :half]

    def score_pieces(hd, t, c, out):
        q0 = t * tq
        q = q_ref[q0:q0 + tq, head_cols(hd)]
        lane = lax.broadcasted_iota(jnp.int32, q.shape, 1)
        mine = (lane < DIFF_HEAD_DIM) if c == 0 else (lane >= DIFF_HEAD_DIM)
        qc = jnp.where(mine, q, jnp.zeros_like(q))
        full, mpart = [], None
        for k0 in range(0, q0, KEY_BLOCK):
            k1 = min(k0 + KEY_BLOCK, q0)
            s = scores_t(k_ref[k0:k1, head_cols(hd)], qc)
            mblk = fold8(s, jnp.max)
            mpart = mblk if mpart is None else jnp.maximum(mpart, mblk)
            full.append((k0, k1, s))
            yield
        s_diag = jnp.where(keep_wide, scores_t(k_ref[q0:q0 + half, head_cols(hd)], qc), -jnp.inf)
        s_last = jnp.where(keep_narrow,
                           scores_t(k_ref[q0 + half:q0 + tq, head_cols(hd)], qc[half:]), -jnp.inf)
        mblk = fold8(s_diag, jnp.max)
        mpart = mblk if mpart is None else jnp.maximum(mpart, mblk)
        mpart = upper_cols(mpart, fold8(s_last, jnp.max), jnp.maximum)
        out.update(full=full, s_diag=s_diag, s_last=s_last,
                   m=jnp.max(mpart, axis=0, keepdims=True))
        yield

    def value_pieces(hd, t, scores, out):
        q0 = t * tq
        m = scores["m"]
        acc, lpart = None, None
        for k0, k1, s in scores["full"]:
            p = jnp.exp2(s - m)
            lblk = fold8(p, jnp.sum)
            lpart = lblk if lpart is None else lpart + lblk
            part = jnp.dot(vt_ref[hd, :, k0:k1], p.astype(BF16), preferred_element_type=F32)
            acc = part if acc is None else acc + part
            yield
        p_diag = jnp.exp2(scores["s_diag"] - m)
        p_last = jnp.exp2(scores["s_last"] - m[:, half:])
        lblk = fold8(p_diag, jnp.sum)
        lpart = lblk if lpart is None else lpart + lblk
        lpart = upper_cols(lpart, fold8(p_last, jnp.sum), jnp.add)
        part = jnp.dot(vt_ref[hd, :, q0:q0 + half], p_diag.astype(BF16), preferred_element_type=F32)
        acc = part if acc is None else acc + part
        acc = upper_cols(acc, jnp.dot(vt_ref[hd, :, q0 + half:q0 + tq], p_last.astype(BF16),
                                      preferred_element_type=F32), jnp.add)
        out.update(acc=acc, l=jnp.sum(lpart, axis=0, keepdims=True))
        yield

    def finish_tile(hd, t, comp1, comp2):
        o = comp1["acc"] * (1.0 / comp1["l"]) - comp2["acc"] * (lam / comp2["l"])
        o = o * lax.rsqrt(jnp.mean(o * o, axis=0, keepdims=True) + NORM_EPS)
        o_ref[t * tq:(t + 1) * tq, head_cols(hd)] = (o.T * sub_ln_gain * (1.0 - lam0)).astype(BF16)

    def emit_in_turn(*stages):
        pending = list(stages)
        while pending:
            for g in list(pending):
                if next(g, StopIteration) is StopIteration:
                    pending.remove(g)

    n_tiles = seq // tq
    n_sections = len(sections)
    for section in range(n_sections):
        emit_in_turn(project(0, section))
    units = [(hd, t, c) for t in range(n_tiles) for hd in range(HEADS_PER_STEP) for c in range(2)]
    assert len(units) // n_tiles == n_sections
    scores = {}
    emit_in_turn(score_pieces(*units[0], scores))
    done = {}
    for i, (hd, t, c) in enumerate(units):
        nxt = {}
        done[c] = {}
        stages = [score_pieces(*units[i + 1], nxt)] if i + 1 < len(units) else []
        stages.append(value_pieces(hd, t, scores, done[c]))
        if t + 1 < n_tiles:
            stages.append(project(t + 1, i % n_sections))
        emit_in_turn(*stages)
        scores = nxt
        if c == 1:
            finish_tile(hd, t, done[0], done[1])


def _proj_attn(x2d, g, w_in, lam_qk, subln_g, later_weights, layer, *, batch, seq, lam0):
    n, d = x2d.shape
    groups = N_DIFF_HEADS // HEADS_PER_STEP
    width = HEADS_PER_STEP * DIFF_V_DIM
    specs = [_slab_specs(w.shape[1], w.shape[2], batch * groups, layer,
                         lambda h, b: h * batch + b) for w in later_weights]

    def w_section(section):
        return pl.BlockSpec((None, d, width), lambda h, b: (layer, 0, section * groups + h),
                            pipeline_mode=pl.Buffered(1))

    outs = pl.pallas_call(
        functools.partial(_proj_attn_kernel, lam0=lam0, layer=layer, n_cast=len(later_weights)),
        grid=(groups, batch),
        in_specs=[
            pl.BlockSpec((seq, d), lambda h, b: (b, 0)),
            _resident(g.shape),
            w_section(0), w_section(1), w_section(2), w_section(3),
            _layer_resident(lam_qk.shape[1:], layer),
            _resident(subln_g.shape),
            *(spec_in for spec_in, _ in specs),
        ],
        out_specs=[pl.BlockSpec((seq, width), lambda h, b: (b, h)),
                   pl.BlockSpec((seq, width), lambda h, b: (b, h)),
                   *(spec_out for _, spec_out in specs)],
        out_shape=[jax.ShapeDtypeStruct((n, ATTN_WIDTH), BF16),
                   jax.ShapeDtypeStruct((n, groups * width), BF16),
                   *(jax.ShapeDtypeStruct(w.shape[1:], BF16) for w in later_weights)],
        scratch_shapes=[
            pltpu.VMEM((seq, width), BF16),
            pltpu.VMEM((seq, width), BF16),
            pltpu.VMEM((HEADS_PER_STEP, DIFF_V_DIM, seq), BF16),
        ],
        compiler_params=pltpu.CompilerParams(
            dimension_semantics=("arbitrary", "arbitrary"), vmem_limit_bytes=VMEM_LIMIT_BYTES),
        name="proj_attention",
    )(x2d, g, w_in, w_in, w_in, w_in, lam_qk, subln_g, *later_weights)
    return outs[0], outs[1], outs[2:]


def _mix_ffn_kernel(x_ref, a_ref, u_ref, halo_ref, pw_ref, ps_ref, wo_ref, g2_ref,
                    wg_ref, wu_ref, wd_ref, gf_ref, o_ref, wpool_ref,
                    *, layer, tiles_per_seq, final_norm):
    tm = TOKEN_TILE
    attn_width = a_ref.shape[1]

    @pl.when(pl.program_id(0) == 0)
    def _():
        for g in range(len(POOL_WINDOWS)):
            rows = slice(g * POOL_GROUP_DIM, (g + 1) * POOL_GROUP_DIM)
            scaled = (pw_ref[rows, :].astype(F32) * ps_ref[layer:layer + 1, rows]).astype(BF16)
            wpool_ref[rows, :] = jnp.dot(
                scaled, wo_ref[attn_width + g * POOL_GROUP_DIM:attn_width + (g + 1) * POOL_GROUP_DIM, :],
                preferred_element_type=F32).astype(BF16)

    def token_tile(sub, carry):
        _mix_ffn_tile(sub, x_ref, a_ref, u_ref, halo_ref, wpool_ref, wo_ref, g2_ref, wg_ref, wu_ref,
                      wd_ref, gf_ref, o_ref, layer=layer, tiles_per_seq=tiles_per_seq,
                      final_norm=final_norm)
        return carry

    lax.fori_loop(0, x_ref.shape[0] // tm, token_tile, 0)


def _mix_ffn_tile(sub, x_ref, a_ref, u_ref, halo_ref, wpool_ref, wo_ref, g2_ref, wg_ref, wu_ref,
                  wd_ref, gf_ref, o_ref, *, layer, tiles_per_seq, final_norm):
    tm = TOKEN_TILE
    tiles_per_step = x_ref.shape[0] // tm
    attn_width = a_ref.shape[1]
    row0 = pl.multiple_of(sub * tm, tm)
    rows = pl.ds(row0, tm)
    t0 = ((pl.program_id(0) * tiles_per_step + sub) % tiles_per_seq) * tm

    d_model = x_ref.shape[1]
    col_step = d_model // len(POOL_WINDOWS)
    a_parts = []

    u = u_ref[rows, :].astype(F32)
    prev0 = pl.multiple_of(jnp.maximum(row0 - MAX_WINDOW, 0), MAX_WINDOW)
    halo = jnp.where(sub == 0, halo_ref[...], u_ref[pl.ds(prev0, MAX_WINDOW), :]).astype(F32)
    ext = jnp.concatenate([jnp.where(t0 == 0, jnp.zeros_like(halo), halo), u], axis=0)
    pos = t0 + lax.broadcasted_iota(jnp.int32, (tm, 1), 0)
    resids = []
    for g, w in enumerate(POOL_WINDOWS):
        a_parts.append(jnp.dot(a_ref[rows, :], wo_ref[:attn_width, g * col_step:(g + 1) * col_step],
                               preferred_element_type=F32))
        cols = slice(g * POOL_GROUP_DIM, (g + 1) * POOL_GROUP_DIM)
        wsum = ext[:, cols]
        span = 1
        while span < w:
            wsum = wsum + pltpu.roll(wsum, shift=span, axis=0)
            span *= 2
        inv_count = 1.0 / jnp.minimum(pos + 1, w).astype(F32)
        resids.append((wsum[MAX_WINDOW:] * inv_count - u[:, cols]).astype(BF16))
    x1 = (x_ref[rows, :] + jnp.concatenate(a_parts, axis=-1)
          + jnp.dot(jnp.concatenate(resids, axis=-1), wpool_ref[...], preferred_element_type=F32))

    h2 = (x1 * g2_ref[layer:layer + 1, :]).astype(BF16)
    scale = _rms_scale(x1)
    ffn = None
    f0 = 0
    for fc in FF_CHUNKS:
        acts = []
        for c0 in range(f0, f0 + fc, MXU_TILE):
            w_gu = jnp.concatenate([wg_ref[:, c0:c0 + MXU_TILE], wu_ref[:, c0:c0 + MXU_TILE]], axis=1)
            gu = jnp.dot(h2, w_gu, preferred_element_type=F32) * scale
            gate, up = gu[:, :MXU_TILE], gu[:, MXU_TILE:]
            acts.append((gate * (1.0 / (1.0 + jnp.exp(-gate))) * up).astype(BF16))
        act = jnp.concatenate(acts, axis=1)
        part = jnp.dot(act, wd_ref[f0:f0 + fc, :], preferred_element_type=F32)
        ffn = part if ffn is None else ffn + part
        f0 += fc
    x2 = x1 + ffn
    if final_norm:
        x2 = x2 * _rms_scale(x2) * gf_ref[...]
    o_ref[rows, :] = x2


def _mix_ffn(x2d, attn, u_all, pool_w, pool_scale, w_out, g2, w_gate, w_up, w_down, final_g,
             layer, *, seq, final_norm):
    n, d = x2d.shape
    d_ff = w_gate.shape[1]
    assert sum(FF_CHUNKS) == d_ff
    assert all(w & (w - 1) == 0 for w in POOL_WINDOWS), "doubling needs power-of-two windows"
    pool_width = len(POOL_WINDOWS) * POOL_GROUP_DIM
    tm = MIX_TILES_PER_STEP * TOKEN_TILE
    halo_per_tile = tm // MAX_WINDOW
    return pl.pallas_call(
        functools.partial(_mix_ffn_kernel, layer=layer, tiles_per_seq=seq // TOKEN_TILE,
                          final_norm=final_norm),
        grid=(n // tm,),
        in_specs=[
            pl.BlockSpec((tm, d), lambda i: (i, 0)),
            pl.BlockSpec((tm, attn.shape[1]), lambda i: (i, 0)),
            pl.BlockSpec((tm, pool_width), lambda i: (i, 0)),
            pl.BlockSpec((MAX_WINDOW, pool_width),
                         lambda i: (jnp.maximum(i * halo_per_tile - 1, 0), 0)),
            _resident(pool_w.shape),
            _resident(pool_scale.shape),
            _resident(w_out.shape),
            _resident(g2.shape),
            _resident(w_gate.shape),
            _resident(w_up.shape),
            _resident(w_down.shape),
            _resident((1, d)),
        ],
        out_specs=pl.BlockSpec((tm, d), lambda i: (i, 0)),
        out_shape=jax.ShapeDtypeStruct((n, d), F32),
        scratch_shapes=[pltpu.VMEM((pool_width, d), BF16)],
        compiler_params=pltpu.CompilerParams(
            dimension_semantics=("arbitrary",), vmem_limit_bytes=VMEM_LIMIT_BYTES),
        name="mix_ffn",
    )(x2d, attn, u_all, u_all, pool_w, pool_scale, w_out, g2, w_gate, w_up, w_down, final_g)


def kernel(x, norm1_g, w_in, lam_qk, subln_g, pool_w, pool_scale, w_out, norm2_g, w_gate, w_up,
           w_down, final_g):
    batch, seq, d = x.shape
    depth = w_in.shape[0]
    assert seq % (MIX_TILES_PER_STEP * TOKEN_TILE) == 0 and seq % Q_TILE == 0
    x2d = x.reshape(batch * seq, d)
    later = (pool_w.reshape(depth, -1, pool_w.shape[-1]), w_out, w_gate, w_up, w_down)
    final_g = final_g.reshape(1, d)
    for layer in range(depth):
        attn, u_all, (pw, wo, wg, wu, wd) = _proj_attn(
            x2d, norm1_g, w_in, lam_qk, subln_g, later, layer, batch=batch, seq=seq,
            lam0=_lambda_init(layer))
        x2d = _mix_ffn(x2d, attn, u_all, pw, pool_scale, wo, norm2_g, wg, wu, wd, final_g, layer,
                       seq=seq, final_norm=(layer == depth - 1))
    return x2d.reshape(batch, seq, d)
```

```python
import functools
import math

import jax
import jax.numpy as jnp
from jax import lax
from jax.experimental import pallas as pl
from jax.experimental.pallas import tpu as pltpu

F32 = jnp.float32
BF16 = jnp.bfloat16

N_DIFF_HEADS = 4
DIFF_HEAD_DIM = 64
DIFF_V_DIM = 2 * DIFF_HEAD_DIM
ATTN_WIDTH = N_DIFF_HEADS * DIFF_V_DIM
Q_SCALE = DIFF_HEAD_DIM ** -0.5 * math.log2(math.e)
POOL_WINDOWS = (2, 4, 8, 16)
POOL_GROUP_DIM = 128
MAX_WINDOW = max(POOL_WINDOWS)
NORM_EPS = 1e-5

F32_SUBLANES = 8
BF16_SUBLANE_TILE = 16
MXU_TILE = 256

VMEM_LIMIT_BYTES = 60 * 1024 * 1024

TOKEN_TILE = 512
MIX_TILES_PER_STEP = 2
Q_TILE = 512
KEY_BLOCK = 512
HEADS_PER_STEP = 2
FF_CHUNKS = (1536, 1280)


def _lambda_init(layer):
    return 0.8 - 0.6 * math.exp(-0.3 * layer)


def _rms_scale(x):
    return lax.rsqrt(jnp.mean(x * x, axis=-1, keepdims=True) + NORM_EPS)


def _resident(shape):
    return pl.BlockSpec(shape, lambda *_: (0,) * len(shape), pipeline_mode=pl.Buffered(1))


def _layer_resident(shape, layer):
    return pl.BlockSpec((None, *shape), lambda *_: (layer,) + (0,) * len(shape),
                        pipeline_mode=pl.Buffered(1))


def _slab_specs(rows, cols, n_steps, layer, step_of):
    hold = next(h for h in range(1, n_steps + 1)
                if n_steps % h == 0 and rows % (n_steps // h) == 0
                and (rows // (n_steps // h)) % BF16_SUBLANE_TILE == 0)
    slab = rows // (n_steps // hold)
    return (pl.BlockSpec((None, slab, cols), lambda *ids: (layer, step_of(*ids) // hold, 0)),
            pl.BlockSpec((slab, cols), lambda *ids: (step_of(*ids) // hold, 0)))


def _proj_attn_kernel(x_ref, g_ref, wq_ref, wk_ref, wv_ref, wu_ref, lamqk_ref, sg_ref, *rest,
                      lam0, layer, n_cast):
    cast_in = rest[:n_cast]
    o_ref, u_ref = rest[n_cast], rest[n_cast + 1]
    cast_out = rest[n_cast + 2:2 * n_cast + 2]
    q_ref, k_ref, vt_ref = rest[2 * n_cast + 2:]
    for src, dst in zip(cast_in, cast_out):
        dst[...] = src[...].astype(BF16)

    seq = x_ref.shape[0]
    tq = Q_TILE
    half = tq // 2

    sections = [w[...].astype(BF16) for w in (wq_ref, wk_ref, wv_ref, wu_ref)]

    def head_cols(hd):
        return slice(hd * DIFF_V_DIM, (hd + 1) * DIFF_V_DIM)

    normed = {}

    def normalised(t):
        if t not in normed:
            x = x_ref[t * tq:(t + 1) * tq, :]
            normed[t] = ((x * g_ref[layer:layer + 1, :]).astype(BF16), _rms_scale(x))
        return normed[t]

    def project(t, section):
        rows = slice(t * tq, (t + 1) * tq)
        h, scale = normalised(t)
        out = jnp.dot(h, sections[section], preferred_element_type=F32) * scale
        if section == 0:
            q_ref[rows, :] = (out * Q_SCALE).astype(BF16)
        elif section == 1:
            k_ref[rows, :] = out.astype(BF16)
        elif section == 2:
            for hd in range(HEADS_PER_STEP):
                vt_ref[hd, :, rows] = out[:, head_cols(hd)].T.astype(BF16)
        else:
            u_ref[rows, :] = out.astype(BF16)
        yield

    sub_ln_gain = sg_ref[layer:layer + 1, :]
    lq = lamqk_ref[...]
    lam = (jnp.exp(jnp.sum(lq[0:1] * lq[1:2], axis=-1, keepdims=True))
           - jnp.exp(jnp.sum(lq[2:3] * lq[3:4], axis=-1, keepdims=True)) + lam0)

    def scores_t(kb, qc):
        return lax.dot_general(kb, qc, (((1,), (1,)), ((), ())), preferred_element_type=F32)

    def fold8(x, op):
        return op(x.reshape(-1, F32_SUBLANES, x.shape[-1]), axis=0)

    def upper_cols(wide, narrow, op):
        return jnp.concatenate([wide[:, :half], op(wide[:, half:], narrow)], axis=1)

    keep_wide = (lax.broadcasted_iota(jnp.int32, (half, tq), 0)
                 <= lax.broadcasted_iota(jnp.int32, (half, tq), 1))
    keep_narrow = keep_wide[:, :half]

    def score_pieces(hd, t, c, out):
        q0 = t * tq
        q = q_ref[q0:q0 + tq, head_cols(hd)]
        lane = lax.broadcasted_iota(jnp.int32, q.shape, 1)
        mine = (lane < DIFF_HEAD_DIM) if c == 0 else (lane >= DIFF_HEAD_DIM)
        qc = jnp.where(mine, q, jnp.zeros_like(q))
        full, mpart = [], None
        for k0 in range(0, q0, KEY_BLOCK):
            k1 = min(k0 + KEY_BLOCK, q0)
            s = scores_t(k_ref[k0:k1, head_cols(hd)], qc)
            mblk = fold8(s, jnp.max)
            mpart = mblk if mpart is None else jnp.maximum(mpart, mblk)
            full.append((k0, k1, s))
            yield
        s_diag = jnp.where(keep_wide, scores_t(k_ref[q0:q0 + half, head_cols(hd)], qc), -jnp.inf)
        s_last = jnp.where(keep_narrow,
                           scores_t(k_ref[q0 + half:q0 + tq, head_cols(hd)], qc[half:]), -jnp.inf)
        mblk = fold8(s_diag, jnp.max)
        mpart = mblk if mpart is None else jnp.maximum(mpart, mblk)
        mpart = upper_cols(mpart, fold8(s_last, jnp.max), jnp.maximum)
        out.update(full=full, s_diag=s_diag, s_last=s_last,
                   m=jnp.max(mpart, axis=0, keepdims=True))
        yield

    def value_pieces(hd, t, scores, out):
        q0 = t * tq
        m = scores["m"]
        acc, lpart = None, None
        for k0, k1, s in scores["full"]:
            p = jnp.exp2(s - m)
            lblk = fold8(p, jnp.sum)
            lpart = lblk if lpart is None else lpart + lblk
            part = jnp.dot(vt_ref[hd, :, k0:k1], p.astype(BF16), preferred_element_type=F32)
            acc = part if acc is None else acc + part
            yield
        p_diag = jnp.exp2(scores["s_diag"] - m)
        p_last = jnp.exp2(scores["s_last"] - m[:, half:])
        lblk = fold8(p_diag, jnp.sum)
        lpart = lblk if lpart is None else lpart + lblk
        lpart = upper_cols(lpart, fold8(p_last, jnp.sum), jnp.add)
        part = jnp.dot(vt_ref[hd, :, q0:q0 + half], p_diag.astype(BF16), preferred_element_type=F32)
        acc = part if acc is None else acc + part
        acc = upper_cols(acc, jnp.dot(vt_ref[hd, :, q0 + half:q0 + tq], p_last.astype(BF16),
                                      preferred_element_type=F32), jnp.add)
        out.update(acc=acc, l=jnp.sum(lpart, axis=0, keepdims=True))
        yield

    def finish_tile(hd, t, comp1, comp2):
        o = comp1["acc"] * (1.0 / comp1["l"]) - comp2["acc"] * (lam / comp2["l"])
        o = o * lax.rsqrt(jnp.mean(o * o, axis=0, keepdims=True) + NORM_EPS)
        o_ref[t * tq:(t + 1) * tq, head_cols(hd)] = (o.T * sub_ln_gain * (1.0 - lam0)).astype(BF16)

    def emit_in_turn(*stages):
        pending = list(stages)
        while pending:
            for g in list(pending):
                if next(g, StopIteration) is StopIteration:
                    pending.remove(g)

    n_tiles = seq // tq
    n_sections = len(sections)
    for section in range(n_sections):
        emit_in_turn(project(0, section))
    units = [(hd, t, c) for t in range(n_tiles) for hd in range(HEADS_PER_STEP) for c in range(2)]
    assert len(units) // n_tiles == n_sections
    scores = {}
    emit_in_turn(score_pieces(*units[0], scores))
    done = {}
    for i, (hd, t, c) in enumerate(units):
        nxt = {}
        done[c] = {}
        stages = [score_pieces(*units[i + 1], nxt)] if i + 1 < len(units) else []
        stages.append(value_pieces(hd, t, scores, done[c]))
        if t + 1 < n_tiles:
            stages.append(project(t + 1, i % n_sections))
        emit_in_turn(*stages)
        scores = nxt
        if c == 1:
            finish_tile(hd, t, done[0], done[1])


def _proj_attn(x2d, g, w_in, lam_qk, subln_g, later_weights, layer, *, batch, seq, lam0):
    n, d = x2d.shape
    groups = N_DIFF_HEADS // HEADS_PER_STEP
    width = HEADS_PER_STEP * DIFF_V_DIM
    specs = [_slab_specs(w.shape[1], w.shape[2], batch * groups, layer,
                         lambda h, b: h * batch + b) for w in later_weights]

    def w_section(section):
        return pl.BlockSpec((None, d, width), lambda h, b: (layer, 0, section * groups + h))

    outs = pl.pallas_call(
        functools.partial(_proj_attn_kernel, lam0=lam0, layer=layer, n_cast=len(later_weights)),
        grid=(groups, batch),
        in_specs=[
            pl.BlockSpec((seq, d), lambda h, b: (b, 0)),
            _resident(g.shape),
            w_section(0), w_section(1), w_section(2), w_section(3),
            _layer_resident(lam_qk.shape[1:], layer),
            _resident(subln_g.shape),
            *(spec_in for spec_in, _ in specs),
        ],
        out_specs=[pl.BlockSpec((seq, width), lambda h, b: (b, h)),
                   pl.BlockSpec((seq, width), lambda h, b: (b, h)),
                   *(spec_out for _, spec_out in specs)],
        out_shape=[jax.ShapeDtypeStruct((n, ATTN_WIDTH), BF16),
                   jax.ShapeDtypeStruct((n, groups * width), BF16),
                   *(jax.ShapeDtypeStruct(w.shape[1:], BF16) for w in later_weights)],
        scratch_shapes=[
            pltpu.VMEM((seq, width), BF16),
            pltpu.VMEM((seq, width), BF16),
            pltpu.VMEM((HEADS_PER_STEP, DIFF_V_DIM, seq), BF16),
        ],
        compiler_params=pltpu.CompilerParams(
            dimension_semantics=("parallel", "parallel"), vmem_limit_bytes=VMEM_LIMIT_BYTES),
        name="proj_attention",
    )(x2d, g, w_in, w_in, w_in, w_in, lam_qk, subln_g, *later_weights)
    return outs[0], outs[1], outs[2:]


def _mix_ffn_kernel(x_ref, a_ref, u_ref, halo_ref, pw_ref, ps_ref, wo_ref, g2_ref,
                    wg_ref, wu_ref, wd_ref, gf_ref, o_ref, wpool_ref,
                    *, layer, tiles_per_seq, final_norm):
    tm = TOKEN_TILE
    attn_width = a_ref.shape[1]

    @pl.when(pl.program_id(0) == 0)
    def _():
        for g in range(len(POOL_WINDOWS)):
            rows = slice(g * POOL_GROUP_DIM, (g + 1) * POOL_GROUP_DIM)
            scaled = (pw_ref[rows, :].astype(F32) * ps_ref[layer:layer + 1, rows]).astype(BF16)
            wpool_ref[rows, :] = jnp.dot(
                scaled, wo_ref[attn_width + g * POOL_GROUP_DIM:attn_width + (g + 1) * POOL_GROUP_DIM, :],
                preferred_element_type=F32).astype(BF16)

    def token_tile(sub, carry):
        _mix_ffn_tile(sub, x_ref, a_ref, u_ref, halo_ref, wpool_ref, wo_ref, g2_ref, wg_ref, wu_ref,
                      wd_ref, gf_ref, o_ref, layer=layer, tiles_per_seq=tiles_per_seq,
                      final_norm=final_norm)
        return carry

    lax.fori_loop(0, x_ref.shape[0] // tm, token_tile, 0)


def _mix_ffn_tile(sub, x_ref, a_ref, u_ref, halo_ref, wpool_ref, wo_ref, g2_ref, wg_ref, wu_ref,
                  wd_ref, gf_ref, o_ref, *, layer, tiles_per_seq, final_norm):
    tm = TOKEN_TILE
    tiles_per_step = x_ref.shape[0] // tm
    attn_width = a_ref.shape[1]
    row0 = pl.multiple_of(sub * tm, tm)
    rows = pl.ds(row0, tm)
    t0 = ((pl.program_id(0) * tiles_per_step + sub) % tiles_per_seq) * tm

    d_model = x_ref.shape[1]
    col_step = d_model // len(POOL_WINDOWS)
    a_parts = []

    u = u_ref[rows, :].astype(F32)
    prev0 = pl.multiple_of(jnp.maximum(row0 - MAX_WINDOW, 0), MAX_WINDOW)
    halo = jnp.where(sub == 0, halo_ref[...], u_ref[pl.ds(prev0, MAX_WINDOW), :]).astype(F32)
    ext = jnp.concatenate([jnp.where(t0 == 0, jnp.zeros_like(halo), halo), u], axis=0)
    pos = t0 + lax.broadcasted_iota(jnp.int32, (tm, 1), 0)
    resids = []
    for g, w in enumerate(POOL_WINDOWS):
        a_parts.append(jnp.dot(a_ref[rows, :], wo_ref[:attn_width, g * col_step:(g + 1) * col_step],
                               preferred_element_type=F32))
        cols = slice(g * POOL_GROUP_DIM, (g + 1) * POOL_GROUP_DIM)
        wsum = ext[:, cols]
        span = 1
        while span < w:
            wsum = wsum + pltpu.roll(wsum, shift=span, axis=0)
            span *= 2
        inv_count = 1.0 / jnp.minimum(pos + 1, w).astype(F32)
        resids.append((wsum[MAX_WINDOW:] * inv_count - u[:, cols]).astype(BF16))
    x1 = (x_ref[rows, :] + jnp.concatenate(a_parts, axis=-1)
          + jnp.dot(jnp.concatenate(resids, axis=-1), wpool_ref[...], preferred_element_type=F32))

    h2 = (x1 * g2_ref[layer:layer + 1, :]).astype(BF16)
    scale = _rms_scale(x1)
    ffn = None
    f0 = 0
    for fc in FF_CHUNKS:
        acts = []
        for c0 in range(f0, f0 + fc, MXU_TILE):
            w_gu = jnp.concatenate([wg_ref[:, c0:c0 + MXU_TILE], wu_ref[:, c0:c0 + MXU_TILE]], axis=1)
            gu = jnp.dot(h2, w_gu, preferred_element_type=F32) * scale
            gate, up = gu[:, :MXU_TILE], gu[:, MXU_TILE:]
            acts.append((gate * (1.0 / (1.0 + jnp.exp(-gate))) * up).astype(BF16))
        act = jnp.concatenate(acts, axis=1)
        part = jnp.dot(act, wd_ref[f0:f0 + fc, :], preferred_element_type=F32)
        ffn = part if ffn is None else ffn + part
        f0 += fc
    x2 = x1 + ffn
    if final_norm:
        x2 = x2 * _rms_scale(x2) * gf_ref[...]
    o_ref[rows, :] = x2


def _mix_ffn(x2d, attn, u_all, pool_w, pool_scale, w_out, g2, w_gate, w_up, w_down, final_g,
             layer, *, seq, final_norm):
    n, d = x2d.shape
    d_ff = w_gate.shape[1]
    assert sum(FF_CHUNKS) == d_ff
    assert all(w & (w - 1) == 0 for w in POOL_WINDOWS), "doubling needs power-of-two windows"
    pool_width = len(POOL_WINDOWS) * POOL_GROUP_DIM
    tm = MIX_TILES_PER_STEP * TOKEN_TILE
    halo_per_tile = tm // MAX_WINDOW
    return pl.pallas_call(
        functools.partial(_mix_ffn_kernel, layer=layer, tiles_per_seq=seq // TOKEN_TILE,
                          final_norm=final_norm),
        grid=(n // tm,),
        in_specs=[
            pl.BlockSpec((tm, d), lambda i: (i, 0)),
            pl.BlockSpec((tm, attn.shape[1]), lambda i: (i, 0)),
            pl.BlockSpec((tm, pool_width), lambda i: (i, 0)),
            pl.BlockSpec((MAX_WINDOW, pool_width),
                         lambda i: (jnp.maximum(i * halo_per_tile - 1, 0), 0)),
            _resident(pool_w.shape),
            _resident(pool_scale.shape),
            _resident(w_out.shape),
            _resident(g2.shape),
            _resident(w_gate.shape),
            _resident(w_up.shape),
            _resident(w_down.shape),
            _resident((1, d)),
        ],
        out_specs=pl.BlockSpec((tm, d), lambda i: (i, 0)),
        out_shape=jax.ShapeDtypeStruct((n, d), F32),
        scratch_shapes=[pltpu.VMEM((pool_width, d), BF16)],
        compiler_params=pltpu.CompilerParams(
            dimension_semantics=("arbitrary",), vmem_limit_bytes=VMEM_LIMIT_BYTES),
        name="mix_ffn",
    )(x2d, attn, u_all, u_all, pool_w, pool_scale, w_out, g2, w_gate, w_up, w_down, final_g)


def kernel(x, norm1_g, w_in, lam_qk, subln_g, pool_w, pool_scale, w_out, norm2_g, w_gate, w_up,
           w_down, final_g):
    batch, seq, d = x.shape
    depth = w_in.shape[0]
    assert seq % (MIX_TILES_PER_STEP * TOKEN_TILE) == 0 and seq % Q_TILE == 0
    x2d = x.reshape(batch * seq, d)
    later = (pool_w.reshape(depth, -1, pool_w.shape[-1]), w_out, w_gate, w_up, w_down)
    final_g = final_g.reshape(1, d)
    for layer in range(depth):
        attn, u_all, (pw, wo, wg, wu, wd) = _proj_attn(
            x2d, norm1_g, w_in, lam_qk, subln_g, later, layer, batch=batch, seq=seq,
            lam0=_lambda_init(layer))
        x2d = _mix_ffn(x2d, attn, u_all, pw, pool_scale, wo, norm2_g, wg, wu, wd, final_g, layer,
                       seq=seq, final_norm=(layer == depth - 1))
    return x2d.reshape(batch, seq, d)
```
